```python
import math
import jax, jax.numpy as jnp
from jax import lax
import numpy as np

D_MODEL = 1024
BATCH = 2
SEQ = 8192
DEPTH = 1

D_MIX = D_MODEL
HEAD_DIM = 64
ATTN_WIDTH = D_MIX // 2
N_Q_HEADS = ATTN_WIDTH // HEAD_DIM
N_KV_HEADS = 2
GQA_GROUP = N_Q_HEADS // N_KV_HEADS
WINDOW = 128
BLOCK = 128
CONV_WIDTH = D_MIX - ATTN_WIDTH
CONV_GROUPS = 8
CONV_KERNEL = 31
D_FF = 2816
FFN_KERNEL = 3
Q_COLS = N_Q_HEADS * HEAD_DIM
KV_COLS = N_KV_HEADS * HEAD_DIM
CONV_IN_COLS = 2 * CONV_WIDTH
IN_COLS = Q_COLS + 2 * KV_COLS + CONV_IN_COLS
EPS = 1e-6
NEG_INF = -1e30

kernel_name = "hymba_swa_sink_conformer_conv_hybrid"


def rmsnorm(x, g):
    xf = x.astype(jnp.float32)
    xf = xf * lax.rsqrt(jnp.mean(xf * xf, axis=-1, keepdims=True) + EPS)
    return (xf * g.astype(jnp.float32)).astype(x.dtype)


def group_layernorm(x, g, b, groups):
    shp = x.shape
    xf = x.astype(jnp.float32).reshape(shp[:-1] + (groups, shp[-1] // groups))
    mu = jnp.mean(xf, axis=-1, keepdims=True)
    var = jnp.mean(jnp.square(xf - mu), axis=-1, keepdims=True)
    xf = ((xf - mu) * lax.rsqrt(var + EPS)).reshape(shp)
    return (xf * g.astype(jnp.float32) + b.astype(jnp.float32)).astype(x.dtype)


def causal_dwconv(x, w, b):
    k_len, ch = w.shape
    y = lax.conv_general_dilated(
        x, w[:, None, :].astype(x.dtype), window_strides=(1,), padding=[(k_len - 1, 0)],
        dimension_numbers=("NWC", "WIO", "NWC"), feature_group_count=ch)
    return y + b.astype(x.dtype)


def alibi_slopes(n_heads):
    return 2.0 ** (-8.0 * (np.arange(n_heads, dtype=np.float32) + 1.0) / n_heads)


def swa_gqa_sinks(q, k, v, sinks):
    b_, s_, _, d_ = q.shape
    nb = s_ // BLOCK
    qb = q.reshape(b_, nb, BLOCK, N_KV_HEADS, GQA_GROUP, d_)
    kb = k.reshape(b_, nb, BLOCK, N_KV_HEADS, d_)
    vb = v.reshape(b_, nb, BLOCK, N_KV_HEADS, d_)
    pad = ((0, 0), (1, 0), (0, 0), (0, 0), (0, 0))
    kk = jnp.concatenate([jnp.pad(kb, pad)[:, :-1], kb], axis=2)
    vv = jnp.concatenate([jnp.pad(vb, pad)[:, :-1], vb], axis=2)
    scores = jnp.einsum("bnqkgd,bnskd->bnkgqs", qb, kk).astype(jnp.float32) / math.sqrt(d_)
    qi = jnp.arange(BLOCK)[:, None]
    kj = jnp.arange(2 * BLOCK)[None, :]
    rel = (qi + BLOCK - kj).astype(jnp.float32)
    band = (rel >= 0) & (rel < WINDOW)
    key_ok = (jnp.arange(nb)[:, None] * BLOCK - BLOCK + jnp.arange(2 * BLOCK)[None, :]) >= 0
    valid = band[None] & key_ok[:, None, :]
    slopes = jnp.asarray(alibi_slopes(N_Q_HEADS)).reshape(N_KV_HEADS, GQA_GROUP)
    scores = scores - slopes[:, :, None, None] * rel
    scores = jnp.where(valid[None, :, None, None], scores, NEG_INF)
    sink = jnp.broadcast_to(sinks.astype(jnp.float32).reshape(N_KV_HEADS, GQA_GROUP, 1, 1),
                            scores.shape[:-1] + (1,))
    probs = jax.nn.softmax(jnp.concatenate([scores, sink], axis=-1), axis=-1)[..., :-1]
    out = jnp.einsum("bnkgqs,bnskd->bnqkgd", probs.astype(v.dtype), vv)
    return out.reshape(b_, s_, N_Q_HEADS * d_)


def setup_inputs(seed: int = 0) -> dict:
    key = jax.random.key(seed)
    ks = jax.random.split(key, 20)
    f32 = jnp.float32
    nrm = lambda k, shp, sc: jax.random.normal(k, shp, f32) * sc
    return {
        "x": nrm(ks[0], (BATCH, SEQ, D_MODEL), 1.0),
        "mix_norm_gain": 1.0 + nrm(ks[1], (D_MODEL,), 0.01),
        "w_in": nrm(ks[2], (D_MODEL, IN_COLS), D_MODEL ** -0.5),
        "b_in": nrm(ks[3], (IN_COLS,), 0.01),
        "q_norm_gain": 1.0 + nrm(ks[4], (HEAD_DIM,), 0.01),
        "k_norm_gain": 1.0 + nrm(ks[5], (HEAD_DIM,), 0.01),
        "attn_sinks": nrm(ks[6], (N_Q_HEADS,), 0.5),
        "conv_dw_w": nrm(ks[7], (CONV_KERNEL, CONV_WIDTH), CONV_KERNEL ** -0.5),
        "conv_dw_b": nrm(ks[8], (CONV_WIDTH,), 0.01),
        "conv_norm_gain": 1.0 + nrm(ks[9], (CONV_WIDTH,), 0.01),
        "conv_norm_bias": nrm(ks[10], (CONV_WIDTH,), 0.01),
        "w_out": nrm(ks[11], (D_MIX, D_MODEL), D_MIX ** -0.5),
        "b_out": nrm(ks[12], (D_MODEL,), 0.01),
        "ffn_norm_gain": 1.0 + nrm(ks[13], (D_MODEL,), 0.01),
        "w_up": nrm(ks[14], (D_MODEL, 2 * D_FF), D_MODEL ** -0.5),
        "ffn_dw_w": nrm(ks[15], (FFN_KERNEL, 2 * D_FF), FFN_KERNEL ** -0.5),
        "ffn_dw_b": nrm(ks[16], (2 * D_FF,), 0.01),
        "w_down": nrm(ks[17], (D_FF, D_MODEL), D_FF ** -0.5),
    }


def reference(x, mix_norm_gain, w_in, b_in, q_norm_gain, k_norm_gain, attn_sinks,
              conv_dw_w, conv_dw_b, conv_norm_gain, conv_norm_bias, w_out, b_out,
              ffn_norm_gain, w_up, ffn_dw_w, ffn_dw_b, w_down):
    b_, s_, _ = x.shape
    for _layer in range(DEPTH):
        h = rmsnorm(x, mix_norm_gain)
        proj = h @ w_in + b_in
        q, k, v, conv_in = jnp.split(
            proj, [Q_COLS, Q_COLS + KV_COLS, Q_COLS + 2 * KV_COLS], axis=-1)
        q = rmsnorm(q.reshape(b_, s_, N_Q_HEADS, HEAD_DIM), q_norm_gain)
        k = rmsnorm(k.reshape(b_, s_, N_KV_HEADS, HEAD_DIM), k_norm_gain)
        v = v.reshape(b_, s_, N_KV_HEADS, HEAD_DIM)
        attn_out = swa_gqa_sinks(q, k, v, attn_sinks)
        a, gate = jnp.split(conv_in, 2, axis=-1)
        c = a * jax.nn.sigmoid(gate)
        c = causal_dwconv(c, conv_dw_w, conv_dw_b)
        c = jax.nn.silu(group_layernorm(c, conv_norm_gain, conv_norm_bias, CONV_GROUPS))
        mixed = jnp.concatenate([attn_out, c], axis=-1)
        x = x + mixed @ w_out + b_out
        h = rmsnorm(x, ffn_norm_gain)
        up = causal_dwconv(h @ w_up, ffn_dw_w, ffn_dw_b)
        g, u = jnp.split(up, 2, axis=-1)
        x = x + (jax.nn.silu(g) * u) @ w_down
    return x
```

```python
import functools
import math

import numpy as np
import jax
import jax.numpy as jnp
from jax import lax
from jax.experimental import pallas as pl
from jax.experimental.pallas import tpu as pltpu

F32 = jnp.float32
BF16 = jnp.bfloat16

HEAD_DIM = 64
N_Q_HEADS = 8
N_KV_HEADS = 2
GQA_GROUP = N_Q_HEADS // N_KV_HEADS
BLOCK = 128
CONV_GROUPS = 8
CONV_KERNEL = 31
FFN_KERNEL = 3
EPS = 1e-6
NEG_INF = -1e30

LANES = 128
SUBLANES = 8
CONV_HALO = 32
FFN_HALO = SUBLANES
VMEM_LIMIT_BYTES = 56 * 1024 * 1024

MIX_TILE = 512
FFN_TILE = 512
FFN_CHUNK = 256
CONV_ROWS = 64

ALIBI_SLOPES = tuple(float(2.0 ** (-8.0 * (h + 1.0) / N_Q_HEADS)) for h in range(N_Q_HEADS))


def _sigmoid(v):
    return 1.0 / (1.0 + jnp.exp(-v))


def _rms_rows(v, gain):
    ms = jnp.mean(v * v, axis=-1, keepdims=True)
    return v * lax.rsqrt(ms + EPS) * gain


def _dot(a, b):
    return jnp.dot(a, b, preferred_element_type=F32)


def _mix_kernel(tiles_per_seq, sinks_ref, x_ref, g1_ref, win_ref, bin_ref, gq_ref, gk_ref,
                cw_ref, cb_ref, cg_ref, cbeta_ref, wout_ref, bout_ref, seg_ref,
                o_ref, kx_ref, vx_ref, cbuf_ref, ybuf_ref, attn_ref):
    ts = x_ref.shape[0]
    attn_w = N_Q_HEADS * HEAD_DIM
    kv_w = N_KV_HEADS * HEAD_DIM
    conv_w = cw_ref.shape[1]
    first = (pl.program_id(0) % tiles_per_seq) == 0

    @pl.when(first)
    def _zero_halos():
        kx_ref[:, 0:BLOCK, :] = jnp.zeros((4, BLOCK, LANES), BF16)
        vx_ref[:, 0:BLOCK, :] = jnp.zeros((4, BLOCK, LANES), BF16)
        cbuf_ref[0:CONV_HALO, :] = jnp.zeros((CONV_HALO, conv_w), F32)

    xt = x_ref[...]
    h = _rms_rows(xt, g1_ref[...]).astype(BF16)
    proj = _dot(h, win_ref[...]) + bin_ref[...]
    seg = seg_ref[...]

    q = proj[:, 0:attn_w]
    q_ms = _dot((q * q).astype(BF16), seg)
    qn = q * lax.rsqrt(q_ms + EPS) * (gq_ref[...] * (1.0 / math.sqrt(HEAD_DIM)))
    k2 = proj[:, attn_w:attn_w + kv_w]
    k_ms = _dot((k2 * k2).astype(BF16), seg[0:kv_w, 0:kv_w])
    kn = k2 * lax.rsqrt(k_ms + EPS) * gk_ref[...]
    v2 = proj[:, attn_w + kv_w:attn_w + 2 * kv_w]

    lane = lax.broadcasted_iota(jnp.int32, (ts, LANES), 1)
    low = lane < HEAD_DIM
    for src, dst in ((kn, kx_ref), (v2, vx_ref)):
        swapped = pltpu.roll(src, HEAD_DIM, axis=1)
        dst[0, BLOCK:BLOCK + ts, :] = jnp.where(low, src, 0.0).astype(BF16)
        dst[1, BLOCK:BLOCK + ts, :] = jnp.where(low, 0.0, swapped).astype(BF16)
        dst[2, BLOCK:BLOCK + ts, :] = jnp.where(low, swapped, 0.0).astype(BF16)
        dst[3, BLOCK:BLOCK + ts, :] = jnp.where(low, 0.0, src).astype(BF16)

    qb = qn.astype(BF16)
    qi = lax.broadcasted_iota(jnp.int32, (BLOCK, 2 * BLOCK), 0)
    kj = lax.broadcasted_iota(jnp.int32, (BLOCK, 2 * BLOCK), 1)
    rel_i = qi + BLOCK - kj
    rel = rel_i.astype(F32)
    band = (rel_i >= 0) & (rel_i < BLOCK)
    first_key = jnp.where(first, BLOCK, 0)
    for n in range(ts // BLOCK):
        r0 = n * BLOCK
        valid = band & (kj >= first_key) if n == 0 else band
        for kv in range(N_KV_HEADS):
            c0 = kv * GQA_GROUP * HEAD_DIM
            kx = jnp.concatenate([kx_ref[2 * kv, r0:r0 + 2 * BLOCK, :],
                                  kx_ref[2 * kv + 1, r0:r0 + 2 * BLOCK, :]], axis=0)
            vx = jnp.concatenate([vx_ref[2 * kv, r0:r0 + 2 * BLOCK, :],
                                  vx_ref[2 * kv + 1, r0:r0 + 2 * BLOCK, :]], axis=0)
            qs = jnp.concatenate([qb[r0:r0 + BLOCK, c0:c0 + LANES],
                                  qb[r0:r0 + BLOCK, c0 + LANES:c0 + 2 * LANES]], axis=0)
            s = lax.dot_general(qs, kx, (((1,), (1,)), ((), ())), preferred_element_type=F32)
            pairs = []
            for pair in range(2):
                probs = []
                for e in range(2):
                    head = kv * GQA_GROUP + pair * 2 + e
                    sh = s[pair * BLOCK:(pair + 1) * BLOCK, e * 2 * BLOCK:(e + 1) * 2 * BLOCK]
                    sh = jnp.where(valid, sh - ALIBI_SLOPES[head] * rel, NEG_INF)
                    sink = sinks_ref[head]
                    m = jnp.maximum(jnp.max(sh, axis=-1, keepdims=True), sink)
                    ex = jnp.exp(sh - m)
                    den = jnp.sum(ex, axis=-1, keepdims=True) + jnp.exp(sink - m)
                    probs.append((ex * (1.0 / den)).astype(BF16))
                pairs.append(jnp.concatenate(probs, axis=1))
            o = _dot(jnp.concatenate(pairs, axis=0), vx)
            attn_ref[r0:r0 + BLOCK, c0:c0 + LANES] = o[0:BLOCK].astype(BF16)
            attn_ref[r0:r0 + BLOCK, c0 + LANES:c0 + 2 * LANES] = o[BLOCK:2 * BLOCK].astype(BF16)

    glu_a = proj[:, attn_w + 2 * kv_w:attn_w + 2 * kv_w + conv_w]
    glu_g = proj[:, attn_w + 2 * kv_w + conv_w:attn_w + 2 * kv_w + 2 * conv_w]
    cbuf_ref[CONV_HALO:CONV_HALO + ts, :] = glu_a * _sigmoid(glu_g)
    tap0 = CONV_HALO - (CONV_KERNEL - 1)
    for r0 in range(0, ts, CONV_ROWS):
        acc = jnp.broadcast_to(cb_ref[...], (CONV_ROWS, conv_w))
        for k in range(CONV_KERNEL):
            acc = acc + cw_ref[k:k + 1, :] * cbuf_ref[tap0 + k + r0:tap0 + k + r0 + CONV_ROWS, :]
        ybuf_ref[r0:r0 + CONV_ROWS, :] = acc
    y = ybuf_ref[...]
    y_hi = y.astype(BF16)
    y_lo = (y - y_hi.astype(F32)).astype(BF16)
    mu = _dot(y_hi, seg) + _dot(y_lo, seg)
    d = y - mu
    var = _dot((d * d).astype(BF16), seg)
    yn = d * lax.rsqrt(var + EPS) * cg_ref[...] + cbeta_ref[...]
    conv_out = (yn * _sigmoid(yn)).astype(BF16)

    mixed = jnp.concatenate([attn_ref[...], conv_out], axis=1)
    o_ref[...] = xt + _dot(mixed, wout_ref[...]) + bout_ref[...]

    kx_ref[:, 0:BLOCK, :] = kx_ref[:, ts:ts + BLOCK, :]
    vx_ref[:, 0:BLOCK, :] = vx_ref[:, ts:ts + BLOCK, :]
    cbuf_ref[0:CONV_HALO, :] = cbuf_ref[ts:ts + CONV_HALO, :]


def _ffn_kernel(tiles_per_seq, x_ref, g_ref, wup_ref, dw_ref, db_ref, wdown_ref,
                o_ref, ubuf_ref, act_ref):
    ts = x_ref.shape[0]
    d_ff = wdown_ref.shape[0]
    first = (pl.program_id(0) % tiles_per_seq) == 0

    @pl.when(first)
    def _zero_halo():
        ubuf_ref[0:FFN_HALO, :] = jnp.zeros((FFN_HALO, 2 * d_ff), F32)

    xt = x_ref[...]
    h = _rms_rows(xt, g_ref[...]).astype(BF16)
    tap0 = FFN_HALO - (FFN_KERNEL - 1)

    def conv(c0):
        acc = jnp.broadcast_to(db_ref[:, c0:c0 + FFN_CHUNK], (ts, FFN_CHUNK))
        for k in range(FFN_KERNEL):
            acc = acc + dw_ref[k:k + 1, c0:c0 + FFN_CHUNK] * ubuf_ref[tap0 + k:tap0 + k + ts, c0:c0 + FFN_CHUNK]
        return acc

    for j in range(d_ff // FFN_CHUNK):
        cg, cu = j * FFN_CHUNK, d_ff + j * FFN_CHUNK
        ubuf_ref[FFN_HALO:FFN_HALO + ts, cg:cg + FFN_CHUNK] = _dot(h, wup_ref[:, cg:cg + FFN_CHUNK])
        ubuf_ref[FFN_HALO:FFN_HALO + ts, cu:cu + FFN_CHUNK] = _dot(h, wup_ref[:, cu:cu + FFN_CHUNK])
        gate = conv(cg)
        act_ref[:, cg:cg + FFN_CHUNK] = (gate * _sigmoid(gate) * conv(cu)).astype(BF16)

    o_ref[...] = xt + _dot(act_ref[...], wdown_ref[...])
    ubuf_ref[0:FFN_HALO, :] = ubuf_ref[ts:ts + FFN_HALO, :]


def _const_spec(shape):
    return pl.BlockSpec(shape, lambda i, *_: (0,) * len(shape), pipeline_mode=pl.Buffered(1))


def _segment_mean_matrix(width, group):
    return np.kron(np.eye(width // group), np.full((group, group), 1.0 / group)).astype(np.float32)


def kernel(x, mix_norm_gain, w_in, b_in, q_norm_gain, k_norm_gain, attn_sinks, conv_dw_w, conv_dw_b,
           conv_norm_gain, conv_norm_bias, w_out, b_out, ffn_norm_gain, w_up, ffn_dw_w, ffn_dw_b, w_down):
    batch, seq, d_model = x.shape
    tokens = batch * seq
    attn_w = N_Q_HEADS * HEAD_DIM
    conv_w = conv_dw_w.shape[1]
    d_ff = w_down.shape[0]
    assert seq % MIX_TILE == 0 and seq % FFN_TILE == 0 and d_ff % FFN_CHUNK == 0
    assert conv_w // CONV_GROUPS == HEAD_DIM and attn_w == conv_w

    x2 = x.reshape(tokens, d_model)
    row = lambda v: v.reshape(1, -1).astype(F32)
    seg = jnp.asarray(_segment_mean_matrix(attn_w, HEAD_DIM), BF16)

    mix_in = (x2, row(mix_norm_gain), w_in.astype(BF16), row(b_in),
              row(jnp.tile(q_norm_gain, N_Q_HEADS)), row(jnp.tile(k_norm_gain, N_KV_HEADS)),
              conv_dw_w.astype(F32), row(conv_dw_b), row(conv_norm_gain), row(conv_norm_bias),
              w_out.astype(BF16), row(b_out), seg)
    tile_spec = lambda t: pl.BlockSpec((t, d_model), lambda i, *_: (i, 0))
    x1 = pl.pallas_call(
        functools.partial(_mix_kernel, seq // MIX_TILE),
        grid_spec=pltpu.PrefetchScalarGridSpec(
            num_scalar_prefetch=1,
            grid=(tokens // MIX_TILE,),
            in_specs=[tile_spec(MIX_TILE)] + [_const_spec(a.shape) for a in mix_in[1:]],
            out_specs=tile_spec(MIX_TILE),
            scratch_shapes=[
                pltpu.VMEM((4, BLOCK + MIX_TILE, LANES), BF16),
                pltpu.VMEM((4, BLOCK + MIX_TILE, LANES), BF16),
                pltpu.VMEM((CONV_HALO + MIX_TILE, conv_w), F32),
                pltpu.VMEM((MIX_TILE, conv_w), F32),
                pltpu.VMEM((MIX_TILE, attn_w), BF16),
            ]),
        out_shape=jax.ShapeDtypeStruct((tokens, d_model), F32),
        compiler_params=pltpu.CompilerParams(dimension_semantics=("arbitrary",),
                                             vmem_limit_bytes=VMEM_LIMIT_BYTES),
        name="token_mix",
    )(attn_sinks.astype(F32), *mix_in)

    ffn_in = (x1, row(ffn_norm_gain), w_up.astype(BF16), ffn_dw_w.astype(F32), row(ffn_dw_b),
              w_down.astype(BF16))
    out = pl.pallas_call(
        functools.partial(_ffn_kernel, seq // FFN_TILE),
        grid=(tokens // FFN_TILE,),
        in_specs=[tile_spec(FFN_TILE)] + [_const_spec(a.shape) for a in ffn_in[1:]],
        out_specs=tile_spec(FFN_TILE),
        scratch_shapes=[
            pltpu.VMEM((FFN_HALO + FFN_TILE, 2 * d_ff), F32),
            pltpu.VMEM((FFN_TILE, d_ff), BF16),
        ],
        out_shape=jax.ShapeDtypeStruct((tokens, d_model), F32),
        compiler_params=pltpu.CompilerParams(dimension_semantics=("arbitrary",),
                                             vmem_limit_bytes=VMEM_LIMIT_BYTES),
        name="channel_mix",
    )(*ffn_in)
    return out.reshape(batch, seq, d_model)
```

```python
import functools
import math

import numpy as np
import jax
import jax.numpy as jnp
from jax import lax
from jax.experimental import pallas as pl
from jax.experimental.pallas import tpu as pltpu

F32 = jnp.float32
BF16 = jnp.bfloat16

HEAD_DIM = 64
N_Q_HEADS = 8
N_KV_HEADS = 2
GQA_GROUP = N_Q_HEADS // N_KV_HEADS
BLOCK = 128
CONV_GROUPS = 8
CONV_KERNEL = 31
FFN_KERNEL = 3
EPS = 1e-6
NEG_INF = -1e30

LANES = 128
SUBLANES = 8
CONV_HALO = 32
FFN_HALO = SUBLANES
VMEM_LIMIT_BYTES = 56 * 1024 * 1024

MIX_TILE = 512
FFN_TILE = 512
FFN_CHUNK = 256
PHASES = 4

ALIBI_SLOPES = tuple(float(2.0 ** (-8.0 * (h + 1.0) / N_Q_HEADS)) for h in range(N_Q_HEADS))


def _sigmoid(v):
    return 1.0 / (1.0 + jnp.exp(-v))


def _rms_rows(v, gain):
    ms = jnp.mean(v * v, axis=-1, keepdims=True)
    return v * lax.rsqrt(ms + EPS) * gain


def _dot(a, b):
    return jnp.dot(a, b, preferred_element_type=F32)


def _mix_kernel(tiles_per_seq, sinks_ref, x_ref, g1_ref, win_ref, bin_ref, gq_ref, gk_ref,
                cw_ref, cb_ref, cg_ref, cbeta_ref, wout_ref, bout_ref, seg_ref,
                o_ref, kx_ref, vx_ref, cbuf_ref, ybuf_ref, attn_ref):
    ts = x_ref.shape[0]
    attn_w = N_Q_HEADS * HEAD_DIM
    kv_w = N_KV_HEADS * HEAD_DIM
    conv_w = cw_ref.shape[1]
    first = (pl.program_id(0) % tiles_per_seq) == 0

    @pl.when(first)
    def _zero_halos():
        kx_ref[:, 0:BLOCK, :] = jnp.zeros((4, BLOCK, LANES), BF16)
        vx_ref[:, 0:BLOCK, :] = jnp.zeros((4, BLOCK, LANES), BF16)
        cbuf_ref[:, 0:CONV_HALO, :] = jnp.zeros((conv_w // LANES, CONV_HALO, LANES), F32)

    xt = x_ref[...]
    h = _rms_rows(xt, g1_ref[...]).astype(BF16)
    proj = _dot(h, win_ref[...]) + bin_ref[...]
    seg = seg_ref[...]

    q = proj[:, 0:attn_w]
    q_ms = _dot((q * q).astype(BF16), seg)
    qn = q * lax.rsqrt(q_ms + EPS) * (gq_ref[...] * (1.0 / math.sqrt(HEAD_DIM)))
    k2 = proj[:, attn_w:attn_w + kv_w]
    k_ms = _dot((k2 * k2).astype(BF16), seg[0:kv_w, 0:kv_w])
    kn = k2 * lax.rsqrt(k_ms + EPS) * gk_ref[...]
    v2 = proj[:, attn_w + kv_w:attn_w + 2 * kv_w]

    lane = lax.broadcasted_iota(jnp.int32, (ts, LANES), 1)
    low = lane < HEAD_DIM
    for src, dst in ((kn, kx_ref), (v2, vx_ref)):
        swapped = pltpu.roll(src, HEAD_DIM, axis=1)
        dst[0, BLOCK:BLOCK + ts, :] = jnp.where(low, src, 0.0).astype(BF16)
        dst[1, BLOCK:BLOCK + ts, :] = jnp.where(low, 0.0, swapped).astype(BF16)
        dst[2, BLOCK:BLOCK + ts, :] = jnp.where(low, swapped, 0.0).astype(BF16)
        dst[3, BLOCK:BLOCK + ts, :] = jnp.where(low, 0.0, src).astype(BF16)

    qb = qn.astype(BF16)
    qi = lax.broadcasted_iota(jnp.int32, (BLOCK, 2 * BLOCK), 0)
    kj = lax.broadcasted_iota(jnp.int32, (BLOCK, 2 * BLOCK), 1)
    rel_i = qi + BLOCK - kj
    rel = rel_i.astype(F32)
    band = (rel_i >= 0) & (rel_i < BLOCK)
    first_key = jnp.where(first, BLOCK, 0)
    for n in range(ts // BLOCK):
        r0 = n * BLOCK
        valid = band & (kj >= first_key) if n == 0 else band
        for kv in range(N_KV_HEADS):
            c0 = kv * GQA_GROUP * HEAD_DIM
            kx = jnp.concatenate([kx_ref[2 * kv, r0:r0 + 2 * BLOCK, :],
                                  kx_ref[2 * kv + 1, r0:r0 + 2 * BLOCK, :]], axis=0)
            vx = jnp.concatenate([vx_ref[2 * kv, r0:r0 + 2 * BLOCK, :],
                                  vx_ref[2 * kv + 1, r0:r0 + 2 * BLOCK, :]], axis=0)
            qs = jnp.concatenate([qb[r0:r0 + BLOCK, c0:c0 + LANES],
                                  qb[r0:r0 + BLOCK, c0 + LANES:c0 + 2 * LANES]], axis=0)
            s = lax.dot_general(qs, kx, (((1,), (1,)), ((), ())), preferred_element_type=F32)
            pairs = []
            for pair in range(2):
                probs = []
                for e in range(2):
                    head = kv * GQA_GROUP + pair * 2 + e
                    sh = s[pair * BLOCK:(pair + 1) * BLOCK, e * 2 * BLOCK:(e + 1) * 2 * BLOCK]
                    sh = jnp.where(valid, sh - ALIBI_SLOPES[head] * rel, NEG_INF)
                    sink = sinks_ref[head]
                    m = jnp.maximum(jnp.max(sh, axis=-1, keepdims=True), sink)
                    ex = jnp.exp(sh - m)
                    den = jnp.sum(ex, axis=-1, keepdims=True) + jnp.exp(sink - m)
                    probs.append((ex * (1.0 / den)).astype(BF16))
                pairs.append(jnp.concatenate(probs, axis=1))
            o = _dot(jnp.concatenate(pairs, axis=0), vx)
            attn_ref[r0:r0 + BLOCK, c0:c0 + LANES] = o[0:BLOCK].astype(BF16)
            attn_ref[r0:r0 + BLOCK, c0 + LANES:c0 + 2 * LANES] = o[BLOCK:2 * BLOCK].astype(BF16)

    glu_a = proj[:, attn_w + 2 * kv_w:attn_w + 2 * kv_w + conv_w]
    glu_g = proj[:, attn_w + 2 * kv_w + conv_w:attn_w + 2 * kv_w + 2 * conv_w]
    glu = glu_a * _sigmoid(glu_g)
    n_slabs = conv_w // LANES
    for j in range(n_slabs):
        cbuf_ref[j, CONV_HALO:CONV_HALO + ts, :] = glu[:, j * LANES:(j + 1) * LANES]
    tap0 = CONV_HALO - (CONV_KERNEL - 1)
    rows = ts // PHASES
    for j in range(n_slabs):
        cols = slice(j * LANES, (j + 1) * LANES)
        for p in range(PHASES):
            acc = jnp.broadcast_to(cb_ref[:, cols], (rows, LANES))
            for k in range(CONV_KERNEL):
                acc = acc + cw_ref[k:k + 1, cols] * cbuf_ref[j, pl.ds(tap0 + k + p, rows, stride=PHASES), :]
            ybuf_ref[j, pl.ds(p, rows, stride=PHASES), :] = acc
    y = jnp.concatenate([ybuf_ref[j] for j in range(n_slabs)], axis=1)
    y_hi = y.astype(BF16)
    y_lo = (y - y_hi.astype(F32)).astype(BF16)
    mu = _dot(y_hi, seg) + _dot(y_lo, seg)
    d = y - mu
    var = _dot((d * d).astype(BF16), seg)
    yn = d * lax.rsqrt(var + EPS) * cg_ref[...] + cbeta_ref[...]
    conv_out = (yn * _sigmoid(yn)).astype(BF16)

    mixed = jnp.concatenate([attn_ref[...], conv_out], axis=1)
    o_ref[...] = xt + _dot(mixed, wout_ref[...]) + bout_ref[...]

    kx_ref[:, 0:BLOCK, :] = kx_ref[:, ts:ts + BLOCK, :]
    vx_ref[:, 0:BLOCK, :] = vx_ref[:, ts:ts + BLOCK, :]
    cbuf_ref[:, 0:CONV_HALO, :] = cbuf_ref[:, ts:ts + CONV_HALO, :]


def _ffn_kernel(tiles_per_seq, x_ref, g_ref, wup_ref, dw_ref, db_ref, wdown_ref,
                o_ref, ubuf_ref, act_ref, obuf_ref):
    ts = x_ref.shape[0]
    d_ff, d_model = wdown_ref.shape
    n_slabs = 2 * d_ff // LANES
    first = (pl.program_id(0) % tiles_per_seq) == 0

    @pl.when(first)
    def _zero_halo():
        ubuf_ref[:, 0:FFN_HALO, :] = jnp.zeros((n_slabs, FFN_HALO, LANES), F32)

    xt = x_ref[...]
    h = _rms_rows(xt, g_ref[...]).astype(BF16)
    tap0 = FFN_HALO - (FFN_KERNEL - 1)
    rows = ts // PHASES

    def up_chunk(c0):
        up = _dot(h, wup_ref[:, c0:c0 + FFN_CHUNK])
        for s in range(FFN_CHUNK // LANES):
            ubuf_ref[c0 // LANES + s, FFN_HALO:FFN_HALO + ts, :] = up[:, s * LANES:(s + 1) * LANES]

    def conv(c0):
        slabs = []
        for s in range(FFN_CHUNK // LANES):
            cols = slice(c0 + s * LANES, c0 + (s + 1) * LANES)
            phases = []
            for p in range(PHASES):
                acc = jnp.broadcast_to(db_ref[:, cols], (rows, LANES))
                for k in range(FFN_KERNEL):
                    acc = acc + dw_ref[k:k + 1, cols] * ubuf_ref[c0 // LANES + s,
                                                                   pl.ds(tap0 + k + p, rows, stride=PHASES), :]
                phases.append(acc)
            slabs.append(jnp.concatenate(phases, axis=0))
        return jnp.concatenate(slabs, axis=1)

    for j in range(d_ff // FFN_CHUNK):
        cg, cu = j * FFN_CHUNK, d_ff + j * FFN_CHUNK
        up_chunk(cg)
        up_chunk(cu)
        gate = conv(cg)
        act_ref[:, cg:cg + FFN_CHUNK] = (gate * _sigmoid(gate) * conv(cu)).astype(BF16)

    down = _dot(act_ref[...], wdown_ref[...])
    for s in range(d_model // LANES):
        for p in range(PHASES):
            obuf_ref[s, pl.ds(p, rows, stride=PHASES), :] = down[p * rows:(p + 1) * rows, s * LANES:(s + 1) * LANES]
    o_ref[...] = xt + jnp.concatenate([obuf_ref[s] for s in range(d_model // LANES)], axis=1)
    ubuf_ref[:, 0:FFN_HALO, :] = ubuf_ref[:, ts:ts + FFN_HALO, :]


def _const_spec(shape):
    return pl.BlockSpec(shape, lambda i, *_: (0,) * len(shape), pipeline_mode=pl.Buffered(1))


def _segment_mean_matrix(width, group):
    return np.kron(np.eye(width // group), np.full((group, group), 1.0 / group)).astype(np.float32)


def kernel(x, mix_norm_gain, w_in, b_in, q_norm_gain, k_norm_gain, attn_sinks, conv_dw_w, conv_dw_b,
           conv_norm_gain, conv_norm_bias, w_out, b_out, ffn_norm_gain, w_up, ffn_dw_w, ffn_dw_b, w_down):
    batch, seq, d_model = x.shape
    tokens = batch * seq
    attn_w = N_Q_HEADS * HEAD_DIM
    conv_w = conv_dw_w.shape[1]
    d_ff = w_down.shape[0]
    assert seq % MIX_TILE == 0 and seq % FFN_TILE == 0 and d_ff % FFN_CHUNK == 0
    assert conv_w // CONV_GROUPS == HEAD_DIM and attn_w == conv_w

    x2 = x.reshape(tokens, d_model)
    row = lambda v: v.reshape(1, -1).astype(F32)
    seg = jnp.asarray(_segment_mean_matrix(attn_w, HEAD_DIM), BF16)

    mix_in = (x2, row(mix_norm_gain), w_in.astype(BF16), row(b_in),
              row(jnp.tile(q_norm_gain, N_Q_HEADS)), row(jnp.tile(k_norm_gain, N_KV_HEADS)),
              conv_dw_w.astype(F32), row(conv_dw_b), row(conv_norm_gain), row(conv_norm_bias),
              w_out.astype(BF16), row(b_out), seg)
    tile_spec = lambda t: pl.BlockSpec((t, d_model), lambda i, *_: (i, 0))
    x1 = pl.pallas_call(
        functools.partial(_mix_kernel, seq // MIX_TILE),
        grid_spec=pltpu.PrefetchScalarGridSpec(
            num_scalar_prefetch=1,
            grid=(tokens // MIX_TILE,),
            in_specs=[tile_spec(MIX_TILE)] + [_const_spec(a.shape) for a in mix_in[1:]],
            out_specs=tile_spec(MIX_TILE),
            scratch_shapes=[
                pltpu.VMEM((4, BLOCK + MIX_TILE, LANES), BF16),
                pltpu.VMEM((4, BLOCK + MIX_TILE, LANES), BF16),
                pltpu.VMEM((conv_w // LANES, CONV_HALO + MIX_TILE, LANES), F32),
                pltpu.VMEM((conv_w // LANES, MIX_TILE, LANES), F32),
                pltpu.VMEM((MIX_TILE, attn_w), BF16),
            ]),
        out_shape=jax.ShapeDtypeStruct((tokens, d_model), F32),
        compiler_params=pltpu.CompilerParams(dimension_semantics=("arbitrary",),
                                             vmem_limit_bytes=VMEM_LIMIT_BYTES),
        name="token_mix",
    )(attn_sinks.astype(F32), *mix_in)

    ffn_in = (x1, row(ffn_norm_gain), w_up.astype(BF16), ffn_dw_w.astype(F32), row(ffn_dw_b),
              w_down.astype(BF16))
    out = pl.pallas_call(
        functools.partial(_ffn_kernel, seq // FFN_TILE),
        grid=(tokens // FFN_TILE,),
        in_specs=[tile_spec(FFN_TILE)] + [_const_spec(a.shape) for a in ffn_in[1:]],
        out_specs=tile_spec(FFN_TILE),
        scratch_shapes=[
            pltpu.VMEM((2 * d_ff // LANES, FFN_HALO + FFN_TILE, LANES), F32),
            pltpu.VMEM((FFN_TILE, d_ff), BF16),
            pltpu.VMEM((d_model // LANES, FFN_TILE, LANES), F32),
        ],
        out_shape=jax.ShapeDtypeStruct((tokens, d_model), F32),
        compiler_params=pltpu.CompilerParams(dimension_semantics=("arbitrary",),
                                             vmem_limit_bytes=VMEM_LIMIT_BYTES),
        name="channel_mix",
    )(*ffn_in)
    return out.reshape(batch, seq, d_model)
```

```python
import functools
import math

import numpy as np
import jax
import jax.numpy as jnp
from jax import lax
from jax.experimental import pallas as pl
from jax.experimental.pallas import tpu as pltpu

F32 = jnp.float32
BF16 = jnp.bfloat16

HEAD_DIM = 64
N_Q_HEADS = 8
N_KV_HEADS = 2
GQA_GROUP = N_Q_HEADS // N_KV_HEADS
BLOCK = 128
CONV_GROUPS = 8
CONV_KERNEL = 31
FFN_KERNEL = 3
EPS = 1e-6
NEG_INF = -1e30
LOG2E = math.log2(math.e)

LANES = 128
SUBLANES = 8
BF16_ROWS = 16
CONV_HALO = 32
FFN_HALO = SUBLANES
VMEM_LIMIT_BYTES = 56 * 1024 * 1024

MIX_TILE = 512
FFN_TILE = 512
FFN_CHUNK = 256
PHASES = 4

ALIBI_SLOPES = tuple(float(2.0 ** (-8.0 * (h + 1.0) / N_Q_HEADS)) for h in range(N_Q_HEADS))


def _sigmoid(v):
    return 1.0 / (1.0 + jnp.exp(-v))


def _rms_rows(v, gain):
    ms = jnp.mean(v * v, axis=-1, keepdims=True)
    return v * lax.rsqrt(ms + EPS) * gain


def _dot(a, b):
    return jnp.dot(a, b, preferred_element_type=F32)


def _mix_kernel(tiles_per_seq, sinks_ref, x_ref, g1_ref, win_ref, bin_ref, gq_ref, gk_ref,
                cw_ref, cb_ref, cg_ref, cbeta_ref, wout_ref, bout_ref, seg_ref, wup_ref, wdown_ref,
                o_ref, wup_bf_ref, wdown_bf_ref,
                kx_ref, vx_ref, cbuf_ref, ybuf_ref, attn_ref, bias_ref):
    ts = x_ref.shape[0]
    attn_w = N_Q_HEADS * HEAD_DIM
    kv_w = N_KV_HEADS * HEAD_DIM
    conv_w = cw_ref.shape[1]
    qkv_w = attn_w + 2 * kv_w
    step = pl.program_id(0)
    first = (step % tiles_per_seq) == 0

    @pl.when(step == 0)
    def _fill_bias_table():
        qi = lax.broadcasted_iota(jnp.int32, (BLOCK, 2 * BLOCK), 0)
        kj = lax.broadcasted_iota(jnp.int32, (BLOCK, 2 * BLOCK), 1)
        rel_i = qi + BLOCK - kj
        band = (rel_i >= 0) & (rel_i < BLOCK)
        rel = rel_i.astype(F32)
        for head in range(N_Q_HEADS):
            bias_ref[head] = jnp.where(band, rel * (-ALIBI_SLOPES[head] * LOG2E), NEG_INF)

    @pl.when(first)
    def _zero_halos():
        kx_ref[:, 0:BLOCK, :] = jnp.zeros((4, BLOCK, LANES), BF16)
        vx_ref[:, 0:BLOCK, :] = jnp.zeros((4, BLOCK, LANES), BF16)
        cbuf_ref[:, 0:CONV_HALO, :] = jnp.zeros((conv_w // LANES, CONV_HALO, LANES), F32)

    wup_bf_ref[...] = wup_ref[...].astype(BF16)
    wdown_bf_ref[...] = wdown_ref[...].astype(BF16)

    xt = x_ref[...]
    h = _rms_rows(xt, g1_ref[...]).astype(BF16)
    seg = seg_ref[...]

    glu_in = _dot(h, win_ref[:, qkv_w:]) + bin_ref[:, qkv_w:]
    glu = glu_in[:, 0:conv_w] * _sigmoid(glu_in[:, conv_w:2 * conv_w])
    n_slabs = conv_w // LANES
    for j in range(n_slabs):
        cbuf_ref[j, CONV_HALO:CONV_HALO + ts, :] = glu[:, j * LANES:(j + 1) * LANES]

    qkv = _dot(h, win_ref[:, 0:qkv_w]) + bin_ref[:, 0:qkv_w]
    q = qkv[:, 0:attn_w]
    q_ms = _dot((q * q).astype(BF16), seg)
    qn = q * lax.rsqrt(q_ms + EPS) * (gq_ref[...] * (LOG2E / math.sqrt(HEAD_DIM)))
    k2 = qkv[:, attn_w:attn_w + kv_w]
    k_ms = _dot((k2 * k2).astype(BF16), seg[0:kv_w, 0:kv_w])
    kn = k2 * lax.rsqrt(k_ms + EPS) * gk_ref[...]
    v2 = qkv[:, attn_w + kv_w:attn_w + 2 * kv_w]

    lane = lax.broadcasted_iota(jnp.int32, (ts, LANES), 1)
    low = lane < HEAD_DIM
    for src, dst in ((kn, kx_ref), (v2, vx_ref)):
        swapped = pltpu.roll(src, HEAD_DIM, axis=1)
        dst[0, BLOCK:BLOCK + ts, :] = jnp.where(low, src, 0.0).astype(BF16)
        dst[1, BLOCK:BLOCK + ts, :] = jnp.where(low, 0.0, swapped).astype(BF16)
        dst[2, BLOCK:BLOCK + ts, :] = jnp.where(low, swapped, 0.0).astype(BF16)
        dst[3, BLOCK:BLOCK + ts, :] = jnp.where(low, 0.0, src).astype(BF16)
    qb = qn.astype(BF16)

    kj = lax.broadcasted_iota(jnp.int32, (BLOCK, 2 * BLOCK), 1)
    seq_start_keys = kj >= jnp.where(first, BLOCK, 0)

    def attention(n, kv):
        r0 = n * BLOCK
        c0 = kv * GQA_GROUP * HEAD_DIM
        kx = jnp.concatenate([kx_ref[2 * kv, r0:r0 + 2 * BLOCK, :],
                              kx_ref[2 * kv + 1, r0:r0 + 2 * BLOCK, :]], axis=0)
        vx = jnp.concatenate([vx_ref[2 * kv, r0:r0 + 2 * BLOCK, :],
                              vx_ref[2 * kv + 1, r0:r0 + 2 * BLOCK, :]], axis=0)
        qs = jnp.concatenate([qb[r0:r0 + BLOCK, c0:c0 + LANES],
                              qb[r0:r0 + BLOCK, c0 + LANES:c0 + 2 * LANES]], axis=0)
        s = lax.dot_general(qs, kx, (((1,), (1,)), ((), ())), preferred_element_type=F32)
        pairs = []
        for pair in range(2):
            probs = []
            for e in range(2):
                head = kv * GQA_GROUP + pair * 2 + e
                sh = s[pair * BLOCK:(pair + 1) * BLOCK, e * 2 * BLOCK:(e + 1) * 2 * BLOCK] + bias_ref[head]
                if n == 0:
                    sh = jnp.where(seq_start_keys, sh, NEG_INF)
                sink = sinks_ref[head] * LOG2E
                m = jnp.maximum(jnp.max(sh, axis=-1, keepdims=True), sink)
                ex = jnp.exp2(sh - m)
                den = jnp.sum(ex, axis=-1, keepdims=True) + jnp.exp2(sink - m)
                probs.append((ex * (1.0 / den)).astype(BF16))
            pairs.append(jnp.concatenate(probs, axis=1))
        o = _dot(jnp.concatenate(pairs, axis=0), vx)
        attn_ref[r0:r0 + BLOCK, c0:c0 + LANES] = o[0:BLOCK].astype(BF16)
        attn_ref[r0:r0 + BLOCK, c0 + LANES:c0 + 2 * LANES] = o[BLOCK:2 * BLOCK].astype(BF16)

    tap0 = CONV_HALO - (CONV_KERNEL - 1)
    rows = ts // PHASES

    def depthwise_conv(j):
        cols = slice(j * LANES, (j + 1) * LANES)
        for p in range(PHASES):
            acc = jnp.broadcast_to(cb_ref[:, cols], (rows, LANES))
            for k in range(CONV_KERNEL):
                acc = acc + cw_ref[k:k + 1, cols] * cbuf_ref[j, pl.ds(tap0 + k + p, rows, stride=PHASES), :]
            ybuf_ref[j, pl.ds(p, rows, stride=PHASES), :] = acc

    units = [(n, kv) for n in range(ts // BLOCK) for kv in range(N_KV_HEADS)]
    for u in range(max(len(units), n_slabs)):
        if u < n_slabs:
            depthwise_conv(u)
        if u < len(units):
            attention(*units[u])

    y = jnp.concatenate([ybuf_ref[j] for j in range(n_slabs)], axis=1)
    y_hi = y.astype(BF16)
    y_lo = (y - y_hi.astype(F32)).astype(BF16)
    mu = _dot(y_hi, seg) + _dot(y_lo, seg)
    d = y - mu
    var = _dot((d * d).astype(BF16), seg)
    yn = d * lax.rsqrt(var + EPS) * cg_ref[...] + cbeta_ref[...]
    conv_out = (yn * _sigmoid(yn)).astype(BF16)

    mixed = jnp.concatenate([attn_ref[...], conv_out], axis=1)
    o_ref[...] = xt + _dot(mixed, wout_ref[...]) + bout_ref[...]

    kx_ref[:, 0:BLOCK, :] = kx_ref[:, ts:ts + BLOCK, :]
    vx_ref[:, 0:BLOCK, :] = vx_ref[:, ts:ts + BLOCK, :]
    cbuf_ref[:, 0:CONV_HALO, :] = cbuf_ref[:, ts:ts + CONV_HALO, :]


def _ffn_kernel(tiles_per_seq, x_ref, g_ref, wup_ref, dw_ref, db_ref, wdown_ref,
                o_ref, ubuf_ref, act_ref, obuf_ref):
    ts = x_ref.shape[0]
    d_ff, d_model = wdown_ref.shape
    n_slabs = 2 * d_ff // LANES
    first = (pl.program_id(0) % tiles_per_seq) == 0

    @pl.when(first)
    def _zero_halo():
        ubuf_ref[:, 0:FFN_HALO, :] = jnp.zeros((n_slabs, FFN_HALO, LANES), F32)

    xt = x_ref[...]
    h = _rms_rows(xt, g_ref[...]).astype(BF16)
    tap0 = FFN_HALO - (FFN_KERNEL - 1)
    rows = ts // PHASES

    def up_chunk(c0):
        up = _dot(h, wup_ref[:, c0:c0 + FFN_CHUNK])
        for s in range(FFN_CHUNK // LANES):
            ubuf_ref[c0 // LANES + s, FFN_HALO:FFN_HALO + ts, :] = up[:, s * LANES:(s + 1) * LANES]

    def conv(c0):
        slabs = []
        for s in range(FFN_CHUNK // LANES):
            cols = slice(c0 + s * LANES, c0 + (s + 1) * LANES)
            phases = []
            for p in range(PHASES):
                acc = jnp.broadcast_to(db_ref[:, cols], (rows, LANES))
                for k in range(FFN_KERNEL):
                    acc = acc + dw_ref[k:k + 1, cols] * ubuf_ref[c0 // LANES + s,
                                                                   pl.ds(tap0 + k + p, rows, stride=PHASES), :]
                phases.append(acc)
            slabs.append(jnp.concatenate(phases, axis=0))
        return jnp.concatenate(slabs, axis=1)

    for j in range(d_ff // FFN_CHUNK):
        cg, cu = j * FFN_CHUNK, d_ff + j * FFN_CHUNK
        up_chunk(cg)
        up_chunk(cu)
        gate = conv(cg)
        act_ref[:, cg:cg + FFN_CHUNK] = (gate * _sigmoid(gate) * conv(cu)).astype(BF16)

    down = _dot(act_ref[...], wdown_ref[...])
    for s in range(d_model // LANES):
        for p in range(PHASES):
            obuf_ref[s, pl.ds(p, rows, stride=PHASES), :] = down[p * rows:(p + 1) * rows, s * LANES:(s + 1) * LANES]
    o_ref[...] = xt + jnp.concatenate([obuf_ref[s] for s in range(d_model // LANES)], axis=1)
    ubuf_ref[:, 0:FFN_HALO, :] = ubuf_ref[:, ts:ts + FFN_HALO, :]


def _const_spec(shape):
    return pl.BlockSpec(shape, lambda i, *_: (0,) * len(shape), pipeline_mode=pl.Buffered(1))


def _band_spec(n_rows, n_cols, n_steps):
    band = n_rows // n_steps
    while band % BF16_ROWS:
        n_steps //= 2
        band = n_rows // n_steps
    assert band * n_steps == n_rows
    return pl.BlockSpec((band, n_cols), lambda i, *_: (jnp.minimum(i, n_steps - 1), 0))


def _segment_mean_matrix(width, group):
    return np.kron(np.eye(width // group), np.full((group, group), 1.0 / group)).astype(np.float32)


def kernel(x, mix_norm_gain, w_in, b_in, q_norm_gain, k_norm_gain, attn_sinks, conv_dw_w, conv_dw_b,
           conv_norm_gain, conv_norm_bias, w_out, b_out, ffn_norm_gain, w_up, ffn_dw_w, ffn_dw_b, w_down):
    batch, seq, d_model = x.shape
    tokens = batch * seq
    attn_w = N_Q_HEADS * HEAD_DIM
    conv_w = conv_dw_w.shape[1]
    d_ff = w_down.shape[0]
    assert seq % MIX_TILE == 0 and seq % FFN_TILE == 0 and d_ff % FFN_CHUNK == 0
    assert conv_w // CONV_GROUPS == HEAD_DIM and attn_w == conv_w

    x2 = x.reshape(tokens, d_model)
    row = lambda v: v.reshape(1, -1).astype(F32)
    seg = jnp.asarray(_segment_mean_matrix(attn_w, HEAD_DIM), BF16)

    mix_consts = (row(mix_norm_gain), w_in.astype(BF16), row(b_in),
                  row(jnp.tile(q_norm_gain, N_Q_HEADS)), row(jnp.tile(k_norm_gain, N_KV_HEADS)),
                  conv_dw_w.astype(F32), row(conv_dw_b), row(conv_norm_gain), row(conv_norm_bias),
                  w_out.astype(BF16), row(b_out), seg)
    n_mix = tokens // MIX_TILE
    tile_spec = lambda t: pl.BlockSpec((t, d_model), lambda i, *_: (i, 0))
    wup_band = _band_spec(d_model, 2 * d_ff, n_mix)
    wdown_band = _band_spec(d_ff, d_model, n_mix)
    x1, w_up_bf, w_down_bf = pl.pallas_call(
        functools.partial(_mix_kernel, seq // MIX_TILE),
        grid_spec=pltpu.PrefetchScalarGridSpec(
            num_scalar_prefetch=1,
            grid=(n_mix,),
            in_specs=[tile_spec(MIX_TILE)] + [_const_spec(a.shape) for a in mix_consts]
                     + [wup_band, wdown_band],
            out_specs=[tile_spec(MIX_TILE), wup_band, wdown_band],
            scratch_shapes=[
                pltpu.VMEM((4, BLOCK + MIX_TILE, LANES), BF16),
                pltpu.VMEM((4, BLOCK + MIX_TILE, LANES), BF16),
                pltpu.VMEM((conv_w // LANES, CONV_HALO + MIX_TILE, LANES), F32),
                pltpu.VMEM((conv_w // LANES, MIX_TILE, LANES), F32),
                pltpu.VMEM((MIX_TILE, attn_w), BF16),
                pltpu.VMEM((N_Q_HEADS, BLOCK, 2 * BLOCK), F32),
            ]),
        out_shape=[jax.ShapeDtypeStruct((tokens, d_model), F32),
                   jax.ShapeDtypeStruct(w_up.shape, BF16),
                   jax.ShapeDtypeStruct(w_down.shape, BF16)],
        compiler_params=pltpu.CompilerParams(dimension_semantics=("arbitrary",),
                                             vmem_limit_bytes=VMEM_LIMIT_BYTES),
        name="token_mix",
    )(attn_sinks.astype(F32), x2, *mix_consts, w_up.astype(F32), w_down.astype(F32))

    ffn_in = (x1, row(ffn_norm_gain), w_up_bf, ffn_dw_w.astype(F32), row(ffn_dw_b), w_down_bf)
    out = pl.pallas_call(
        functools.partial(_ffn_kernel, seq // FFN_TILE),
        grid=(tokens // FFN_TILE,),
        in_specs=[tile_spec(FFN_TILE)] + [_const_spec(a.shape) for a in ffn_in[1:]],
        out_specs=tile_spec(FFN_TILE),
        scratch_shapes=[
            pltpu.VMEM((2 * d_ff // LANES, FFN_HALO + FFN_TILE, LANES), F32),
            pltpu.VMEM((FFN_TILE, d_ff), BF16),
            pltpu.VMEM((d_model // LANES, FFN_TILE, LANES), F32),
        ],
        out_shape=jax.ShapeDtypeStruct((tokens, d_model), F32),
        compiler_params=pltpu.CompilerParams(dimension_semantics=("arbitrary",),
                                             vmem_limit_bytes=VMEM_LIMIT_BYTES),
        name="channel_mix",
    )(*ffn_in)
    return out.reshape(batch, seq, d_model)
```

```python
import functools
import math

import numpy as np
import jax
import jax.numpy as jnp
from jax import lax
from jax.experimental import pallas as pl
from jax.experimental.pallas import tpu as pltpu

F32 = jnp.float32
BF16 = jnp.bfloat16

HEAD_DIM = 64
N_Q_HEADS = 8
N_KV_HEADS = 2
GQA_GROUP = N_Q_HEADS // N_KV_HEADS
BLOCK = 128
CONV_GROUPS = 8
CONV_KERNEL = 31
FFN_KERNEL = 3
EPS = 1e-6
NEG_INF = -1e30
LOG2E = math.log2(math.e)

LANES = 128
SUBLANES = 8
BF16_ROWS = 16
CONV_HALO = 32
FFN_HALO = SUBLANES
VMEM_LIMIT_BYTES = 56 * 1024 * 1024

MIX_TILE = 512
FFN_TILE = 512
FFN_CHUNK = 256
PHASES = 4

ALIBI_SLOPES = tuple(float(2.0 ** (-8.0 * (h + 1.0) / N_Q_HEADS)) for h in range(N_Q_HEADS))


def _sigmoid(v):
    return 1.0 / (1.0 + jnp.exp2(v * (-LOG2E)))


def _rms_rows(v, gain):
    ms = jnp.mean(v * v, axis=-1, keepdims=True)
    return v * lax.rsqrt(ms + EPS) * gain


def _dot(a, b):
    return jnp.dot(a, b, preferred_element_type=F32)


def _mix_kernel(tiles_per_seq, sinks_ref, x_ref, g1_ref, win32_ref, bin_ref, gq_ref, gk_ref,
                cw_ref, cb_ref, cg_ref, cbeta_ref, wout32_ref, bout_ref, seg_ref, wup_ref, wdown_ref,
                o_ref, wup_bf_ref, wdown_bf_ref,
                kx_ref, vx_ref, cbuf_ref, ybuf_ref, attn_ref, bias_ref, win_ref, wout_ref):
    ts = x_ref.shape[0]
    attn_w = N_Q_HEADS * HEAD_DIM
    kv_w = N_KV_HEADS * HEAD_DIM
    conv_w = cw_ref.shape[1]
    qkv_w = attn_w + 2 * kv_w
    step = pl.program_id(0)
    first = (step % tiles_per_seq) == 0

    @pl.when(step == 0)
    def _fill_tables():
        win_ref[...] = win32_ref[...].astype(BF16)
        wout_ref[...] = wout32_ref[...].astype(BF16)
        qi = lax.broadcasted_iota(jnp.int32, (BLOCK, 2 * BLOCK), 0)
        kj = lax.broadcasted_iota(jnp.int32, (BLOCK, 2 * BLOCK), 1)
        rel_i = qi + BLOCK - kj
        band = (rel_i >= 0) & (rel_i < BLOCK)
        rel = rel_i.astype(F32)
        for head in range(N_Q_HEADS):
            bias_ref[head] = jnp.where(band, rel * (-ALIBI_SLOPES[head] * LOG2E), NEG_INF)

    @pl.when(first)
    def _zero_halos():
        kx_ref[:, 0:BLOCK, :] = jnp.zeros((4, BLOCK, LANES), BF16)
        vx_ref[:, 0:BLOCK, :] = jnp.zeros((4, BLOCK, LANES), BF16)
        cbuf_ref[:, 0:CONV_HALO, :] = jnp.zeros((conv_w // LANES, CONV_HALO, LANES), F32)

    wup_bf_ref[...] = wup_ref[...].astype(BF16)
    wdown_bf_ref[...] = wdown_ref[...].astype(BF16)

    xt = x_ref[...]
    h = _rms_rows(xt, g1_ref[...]).astype(BF16)
    seg = seg_ref[...]

    glu_in = _dot(h, win_ref[:, qkv_w:]) + bin_ref[:, qkv_w:]
    glu = glu_in[:, 0:conv_w] * _sigmoid(glu_in[:, conv_w:2 * conv_w])
    n_slabs = conv_w // LANES
    for j in range(n_slabs):
        cbuf_ref[j, CONV_HALO:CONV_HALO + ts, :] = glu[:, j * LANES:(j + 1) * LANES]

    qkv = _dot(h, win_ref[:, 0:qkv_w]) + bin_ref[:, 0:qkv_w]
    q = qkv[:, 0:attn_w]
    q_ms = _dot((q * q).astype(BF16), seg)
    qn = q * lax.rsqrt(q_ms + EPS) * (gq_ref[...] * (LOG2E / math.sqrt(HEAD_DIM)))
    k2 = qkv[:, attn_w:attn_w + kv_w]
    k_ms = _dot((k2 * k2).astype(BF16), seg[0:kv_w, 0:kv_w])
    kn = k2 * lax.rsqrt(k_ms + EPS) * gk_ref[...]
    v2 = qkv[:, attn_w + kv_w:attn_w + 2 * kv_w]

    lane = lax.broadcasted_iota(jnp.int32, (ts, LANES), 1)
    low = lane < HEAD_DIM
    for src, dst in ((kn, kx_ref), (v2, vx_ref)):
        swapped = pltpu.roll(src, HEAD_DIM, axis=1)
        dst[0, BLOCK:BLOCK + ts, :] = jnp.where(low, src, 0.0).astype(BF16)
        dst[1, BLOCK:BLOCK + ts, :] = jnp.where(low, 0.0, swapped).astype(BF16)
        dst[2, BLOCK:BLOCK + ts, :] = jnp.where(low, swapped, 0.0).astype(BF16)
        dst[3, BLOCK:BLOCK + ts, :] = jnp.where(low, 0.0, src).astype(BF16)
    qb = qn.astype(BF16)

    kj = lax.broadcasted_iota(jnp.int32, (BLOCK, 2 * BLOCK), 1)
    seq_start_keys = kj >= jnp.where(first, BLOCK, 0)

    def attention(n, kv):
        r0 = n * BLOCK
        c0 = kv * GQA_GROUP * HEAD_DIM
        kx = jnp.concatenate([kx_ref[2 * kv, r0:r0 + 2 * BLOCK, :],
                              kx_ref[2 * kv + 1, r0:r0 + 2 * BLOCK, :]], axis=0)
        vx = jnp.concatenate([vx_ref[2 * kv, r0:r0 + 2 * BLOCK, :],
                              vx_ref[2 * kv + 1, r0:r0 + 2 * BLOCK, :]], axis=0)
        qs = jnp.concatenate([qb[r0:r0 + BLOCK, c0:c0 + LANES],
                              qb[r0:r0 + BLOCK, c0 + LANES:c0 + 2 * LANES]], axis=0)
        s = lax.dot_general(qs, kx, (((1,), (1,)), ((), ())), preferred_element_type=F32)
        pairs = []
        for pair in range(2):
            probs = []
            for e in range(2):
                head = kv * GQA_GROUP + pair * 2 + e
                sh = s[pair * BLOCK:(pair + 1) * BLOCK, e * 2 * BLOCK:(e + 1) * 2 * BLOCK] + bias_ref[head]
                if n == 0:
                    sh = jnp.where(seq_start_keys, sh, NEG_INF)
                sink = sinks_ref[head] * LOG2E
                m = jnp.maximum(jnp.max(sh, axis=-1, keepdims=True), sink)
                ex = jnp.exp2(sh - m)
                den = jnp.sum(ex, axis=-1, keepdims=True) + jnp.exp2(sink - m)
                probs.append((ex * (1.0 / den)).astype(BF16))
            pairs.append(jnp.concatenate(probs, axis=1))
        o = _dot(jnp.concatenate(pairs, axis=0), vx)
        attn_ref[r0:r0 + BLOCK, c0:c0 + LANES] = o[0:BLOCK].astype(BF16)
        attn_ref[r0:r0 + BLOCK, c0 + LANES:c0 + 2 * LANES] = o[BLOCK:2 * BLOCK].astype(BF16)

    tap0 = CONV_HALO - (CONV_KERNEL - 1)
    rows = ts // PHASES

    def depthwise_conv(j):
        cols = slice(j * LANES, (j + 1) * LANES)
        for p in range(PHASES):
            acc = jnp.broadcast_to(cb_ref[:, cols], (rows, LANES))
            for k in range(CONV_KERNEL):
                acc = acc + cw_ref[k:k + 1, cols] * cbuf_ref[j, pl.ds(tap0 + k + p, rows, stride=PHASES), :]
            ybuf_ref[j, pl.ds(p, rows, stride=PHASES), :] = acc

    units = [(n, kv) for n in range(ts // BLOCK) for kv in range(N_KV_HEADS)]
    for u in range(max(len(units), n_slabs)):
        if u < n_slabs:
            depthwise_conv(u)
        if u < len(units):
            attention(*units[u])

    y = jnp.concatenate([ybuf_ref[j] for j in range(n_slabs)], axis=1)
    y_hi = y.astype(BF16)
    y_lo = (y - y_hi.astype(F32)).astype(BF16)
    mu = _dot(y_hi, seg) + _dot(y_lo, seg)
    d = y - mu
    var = _dot((d * d).astype(BF16), seg)
    yn = d * lax.rsqrt(var + EPS) * cg_ref[...] + cbeta_ref[...]
    conv_out = (yn * _sigmoid(yn)).astype(BF16)

    mixed = jnp.concatenate([attn_ref[...], conv_out], axis=1)
    o_ref[...] = xt + _dot(mixed, wout_ref[...]) + bout_ref[...]

    kx_ref[:, 0:BLOCK, :] = kx_ref[:, ts:ts + BLOCK, :]
    vx_ref[:, 0:BLOCK, :] = vx_ref[:, ts:ts + BLOCK, :]
    cbuf_ref[:, 0:CONV_HALO, :] = cbuf_ref[:, ts:ts + CONV_HALO, :]


def _ffn_kernel(tiles_per_seq, x_ref, xnext_ref, g_ref, wup_ref, dw_ref, db_ref, wdown_ref,
                o_ref, ubuf_ref, act_ref, obuf_ref, hbuf_ref):
    ts = x_ref.shape[0]
    d_ff, d_model = wdown_ref.shape
    n_slabs = 2 * d_ff // LANES
    step = pl.program_id(0)
    first = (step % tiles_per_seq) == 0

    @pl.when(first)
    def _zero_halo():
        ubuf_ref[:, 0:FFN_HALO, :] = jnp.zeros((n_slabs, FFN_HALO, LANES), F32)

    @pl.when(step == 0)
    def _first_norm():
        hbuf_ref[...] = _rms_rows(x_ref[...], g_ref[...]).astype(BF16)

    xt = x_ref[...]
    h = hbuf_ref[...]
    tap0 = FFN_HALO - (FFN_KERNEL - 1)
    rows = ts // PHASES

    def up_chunk(c0):
        up = _dot(h, wup_ref[:, c0:c0 + FFN_CHUNK])
        for s in range(FFN_CHUNK // LANES):
            ubuf_ref[c0 // LANES + s, FFN_HALO:FFN_HALO + ts, :] = up[:, s * LANES:(s + 1) * LANES]

    def conv(c0):
        slabs = []
        for s in range(FFN_CHUNK // LANES):
            cols = slice(c0 + s * LANES, c0 + (s + 1) * LANES)
            phases = []
            for p in range(PHASES):
                acc = jnp.broadcast_to(db_ref[:, cols], (rows, LANES))
                for k in range(FFN_KERNEL):
                    acc = acc + dw_ref[k:k + 1, cols] * ubuf_ref[c0 // LANES + s,
                                                                   pl.ds(tap0 + k + p, rows, stride=PHASES), :]
                phases.append(acc)
            slabs.append(jnp.concatenate(phases, axis=0))
        return jnp.concatenate(slabs, axis=1)

    for j in range(d_ff // FFN_CHUNK):
        cg, cu = j * FFN_CHUNK, d_ff + j * FFN_CHUNK
        up_chunk(cg)
        up_chunk(cu)
        gate = conv(cg)
        act_ref[:, cg:cg + FFN_CHUNK] = (gate * _sigmoid(gate) * conv(cu)).astype(BF16)
        if j == 0:
            h_next = _rms_rows(xnext_ref[...], g_ref[...]).astype(BF16)

    down = _dot(act_ref[...], wdown_ref[...])
    for s in range(d_model // LANES):
        for p in range(PHASES):
            obuf_ref[s, pl.ds(p, rows, stride=PHASES), :] = down[p * rows:(p + 1) * rows, s * LANES:(s + 1) * LANES]
    o_ref[...] = xt + jnp.concatenate([obuf_ref[s] for s in range(d_model // LANES)], axis=1)
    ubuf_ref[:, 0:FFN_HALO, :] = ubuf_ref[:, ts:ts + FFN_HALO, :]
    hbuf_ref[...] = h_next


def _const_spec(shape):
    return pl.BlockSpec(shape, lambda i, *_: (0,) * len(shape), pipeline_mode=pl.Buffered(1))


def _band_spec(n_rows, n_cols, n_steps):
    band = n_rows // n_steps
    while band % BF16_ROWS:
        n_steps //= 2
        band = n_rows // n_steps
    assert band * n_steps == n_rows
    return pl.BlockSpec((band, n_cols), lambda i, *_: (jnp.minimum(i, n_steps - 1), 0))


def _segment_mean_matrix(width, group):
    return np.kron(np.eye(width // group), np.full((group, group), 1.0 / group)).astype(np.float32)


def kernel(x, mix_norm_gain, w_in, b_in, q_norm_gain, k_norm_gain, attn_sinks, conv_dw_w, conv_dw_b,
           conv_norm_gain, conv_norm_bias, w_out, b_out, ffn_norm_gain, w_up, ffn_dw_w, ffn_dw_b, w_down):
    batch, seq, d_model = x.shape
    tokens = batch * seq
    attn_w = N_Q_HEADS * HEAD_DIM
    conv_w = conv_dw_w.shape[1]
    d_ff = w_down.shape[0]
    assert seq % MIX_TILE == 0 and seq % FFN_TILE == 0 and d_ff % FFN_CHUNK == 0
    assert conv_w // CONV_GROUPS == HEAD_DIM and attn_w == conv_w

    x2 = x.reshape(tokens, d_model)
    row = lambda v: v.reshape(1, -1).astype(F32)
    seg = jnp.asarray(_segment_mean_matrix(attn_w, HEAD_DIM), BF16)

    mix_consts = (row(mix_norm_gain), w_in.astype(F32), row(b_in),
                  row(jnp.tile(q_norm_gain, N_Q_HEADS)), row(jnp.tile(k_norm_gain, N_KV_HEADS)),
                  conv_dw_w.astype(F32), row(conv_dw_b), row(conv_norm_gain), row(conv_norm_bias),
                  w_out.astype(F32), row(b_out), seg)
    n_mix = tokens // MIX_TILE
    tile_spec = lambda t: pl.BlockSpec((t, d_model), lambda i, *_: (i, 0))
    wup_band = _band_spec(d_model, 2 * d_ff, n_mix)
    wdown_band = _band_spec(d_ff, d_model, n_mix)
    x1, w_up_bf, w_down_bf = pl.pallas_call(
        functools.partial(_mix_kernel, seq // MIX_TILE),
        grid_spec=pltpu.PrefetchScalarGridSpec(
            num_scalar_prefetch=1,
            grid=(n_mix,),
            in_specs=[tile_spec(MIX_TILE)] + [_const_spec(a.shape) for a in mix_consts]
                     + [wup_band, wdown_band],
            out_specs=[tile_spec(MIX_TILE), wup_band, wdown_band],
            scratch_shapes=[
                pltpu.VMEM((4, BLOCK + MIX_TILE, LANES), BF16),
                pltpu.VMEM((4, BLOCK + MIX_TILE, LANES), BF16),
                pltpu.VMEM((conv_w // LANES, CONV_HALO + MIX_TILE, LANES), F32),
                pltpu.VMEM((conv_w // LANES, MIX_TILE, LANES), F32),
                pltpu.VMEM((MIX_TILE, attn_w), BF16),
                pltpu.VMEM((N_Q_HEADS, BLOCK, 2 * BLOCK), F32),
                pltpu.VMEM(w_in.shape, BF16),
                pltpu.VMEM(w_out.shape, BF16),
            ]),
        out_shape=[jax.ShapeDtypeStruct((tokens, d_model), F32),
                   jax.ShapeDtypeStruct(w_up.shape, BF16),
                   jax.ShapeDtypeStruct(w_down.shape, BF16)],
        compiler_params=pltpu.CompilerParams(dimension_semantics=("arbitrary",),
                                             vmem_limit_bytes=VMEM_LIMIT_BYTES),
        name="token_mix",
    )(attn_sinks.astype(F32), x2, *mix_consts, w_up.astype(F32), w_down.astype(F32))

    ffn_consts = (row(ffn_norm_gain), w_up_bf, ffn_dw_w.astype(F32), row(ffn_dw_b), w_down_bf)
    n_ffn = tokens // FFN_TILE
    next_tile_spec = pl.BlockSpec((FFN_TILE, d_model), lambda i: (jnp.minimum(i + 1, n_ffn - 1), 0))
    out = pl.pallas_call(
        functools.partial(_ffn_kernel, seq // FFN_TILE),
        grid=(n_ffn,),
        in_specs=[tile_spec(FFN_TILE), next_tile_spec] + [_const_spec(a.shape) for a in ffn_consts],
        out_specs=tile_spec(FFN_TILE),
        scratch_shapes=[
            pltpu.VMEM((2 * d_ff // LANES, FFN_HALO + FFN_TILE, LANES), F32),
            pltpu.VMEM((FFN_TILE, d_ff), BF16),
            pltpu.VMEM((d_model // LANES, FFN_TILE, LANES), F32),
            pltpu.VMEM((FFN_TILE, d_model), BF16),
        ],
        out_shape=jax.ShapeDtypeStruct((tokens, d_model), F32),
        compiler_params=pltpu.CompilerParams(dimension_semantics=("arbitrary",),
                                             vmem_limit_bytes=VMEM_LIMIT_BYTES),
        name="channel_mix",
    )(x1, x1, *ffn_consts)
    return out.reshape(batch, seq, d_model)
```

```python
import functools
import math

import numpy as np
import jax
import jax.numpy as jnp
from jax import lax
from jax.experimental import pallas as pl
from jax.experimental.pallas import tpu as pltpu

F32 = jnp.float32
BF16 = jnp.bfloat16

HEAD_DIM = 64
N_Q_HEADS = 8
N_KV_HEADS = 2
GQA_GROUP = N_Q_HEADS // N_KV_HEADS
BLOCK = 128
CONV_GROUPS = 8
CONV_KERNEL = 31
FFN_KERNEL = 3
EPS = 1e-6
NEG_INF = -1e30
LOG2E = math.log2(math.e)

LANES = 128
SUBLANES = 8
BF16_ROWS = 16
CONV_HALO = 32
FFN_HALO = SUBLANES
VMEM_LIMIT_BYTES = 60 * 1024 * 1024

MIX_TILE = 1024
FFN_TILE = 512
FFN_CHUNK = 256
PHASES = 4

ALIBI_SLOPES = tuple(float(2.0 ** (-8.0 * (h + 1.0) / N_Q_HEADS)) for h in range(N_Q_HEADS))


def _sigmoid(v):
    return 1.0 / (1.0 + jnp.exp2(v * (-LOG2E)))


def _rms_rows(v, gain):
    ms = jnp.mean(v * v, axis=-1, keepdims=True)
    return v * lax.rsqrt(ms + EPS) * gain


def _dot(a, b):
    return jnp.dot(a, b, preferred_element_type=F32)


def _mix_kernel(tiles_per_seq, sinks_ref, x_ref, vec_ref, win32_ref, cw_ref, wout32_ref, seg_ref,
                wup_ref, wdown_ref,
                o_ref, wup_bf_ref, wdown_bf_ref,
                kx_ref, vx_ref, cbuf_ref, ybuf_ref, attn_ref, bias_ref, win_ref, wout_ref):
    ts = x_ref.shape[0]
    attn_w = N_Q_HEADS * HEAD_DIM
    kv_w = N_KV_HEADS * HEAD_DIM
    conv_w = cw_ref.shape[1]
    qkv_w = attn_w + 2 * kv_w
    d_model = x_ref.shape[1]
    widths = (d_model, qkv_w + 2 * conv_w, attn_w, kv_w, conv_w, conv_w, conv_w, d_model)
    starts = [sum(widths[:n]) for n in range(len(widths))]
    g1_ref, bin_ref, gq_ref, gk_ref, cb_ref, cg_ref, cbeta_ref, bout_ref = (
        vec_ref.at[:, a:a + w] for a, w in zip(starts, widths))
    step = pl.program_id(0)
    first = (step % tiles_per_seq) == 0

    @pl.when(step == 0)
    def _fill_tables():
        win_ref[...] = win32_ref[...].astype(BF16)
        wout_ref[...] = wout32_ref[...].astype(BF16)
        qi = lax.broadcasted_iota(jnp.int32, (BLOCK, 2 * BLOCK), 0)
        kj = lax.broadcasted_iota(jnp.int32, (BLOCK, 2 * BLOCK), 1)
        rel_i = qi + BLOCK - kj
        band = (rel_i >= 0) & (rel_i < BLOCK)
        rel = rel_i.astype(F32)
        for head in range(N_Q_HEADS):
            bias_ref[head] = jnp.where(band, rel * (-ALIBI_SLOPES[head] * LOG2E), NEG_INF)

    @pl.when(first)
    def _zero_halos():
        kx_ref[:, 0:BLOCK, :] = jnp.zeros((4, BLOCK, LANES), BF16)
        vx_ref[:, 0:BLOCK, :] = jnp.zeros((4, BLOCK, LANES), BF16)
        cbuf_ref[:, 0:CONV_HALO, :] = jnp.zeros((conv_w // LANES, CONV_HALO, LANES), F32)

    wup_bf_ref[...] = wup_ref[...].astype(BF16)
    wdown_bf_ref[...] = wdown_ref[...].astype(BF16)

    xt = x_ref[...]
    h = _rms_rows(xt, g1_ref[...]).astype(BF16)
    seg = seg_ref[...]

    glu_in = _dot(h, win_ref[:, qkv_w:]) + bin_ref[:, qkv_w:]
    glu = glu_in[:, 0:conv_w] * _sigmoid(glu_in[:, conv_w:2 * conv_w])
    n_slabs = conv_w // LANES
    for j in range(n_slabs):
        cbuf_ref[j, CONV_HALO:CONV_HALO + ts, :] = glu[:, j * LANES:(j + 1) * LANES]

    qkv = _dot(h, win_ref[:, 0:qkv_w]) + bin_ref[:, 0:qkv_w]
    q = qkv[:, 0:attn_w]
    q_ms = _dot((q * q).astype(BF16), seg)
    qn = q * lax.rsqrt(q_ms + EPS) * (gq_ref[...] * (LOG2E / math.sqrt(HEAD_DIM)))
    k2 = qkv[:, attn_w:attn_w + kv_w]
    k_ms = _dot((k2 * k2).astype(BF16), seg[0:kv_w, 0:kv_w])
    kn = k2 * lax.rsqrt(k_ms + EPS) * gk_ref[...]
    v2 = qkv[:, attn_w + kv_w:attn_w + 2 * kv_w]

    lane = lax.broadcasted_iota(jnp.int32, (ts, LANES), 1)
    low = lane < HEAD_DIM
    for src, dst in ((kn, kx_ref), (v2, vx_ref)):
        swapped = pltpu.roll(src, HEAD_DIM, axis=1)
        dst[0, BLOCK:BLOCK + ts, :] = jnp.where(low, src, 0.0).astype(BF16)
        dst[1, BLOCK:BLOCK + ts, :] = jnp.where(low, 0.0, swapped).astype(BF16)
        dst[2, BLOCK:BLOCK + ts, :] = jnp.where(low, swapped, 0.0).astype(BF16)
        dst[3, BLOCK:BLOCK + ts, :] = jnp.where(low, 0.0, src).astype(BF16)
    qb = qn.astype(BF16)

    kj = lax.broadcasted_iota(jnp.int32, (BLOCK, 2 * BLOCK), 1)
    seq_start_keys = kj >= jnp.where(first, BLOCK, 0)

    def attention(n, kv):
        r0 = n * BLOCK
        c0 = kv * GQA_GROUP * HEAD_DIM
        kx = jnp.concatenate([kx_ref[2 * kv, r0:r0 + 2 * BLOCK, :],
                              kx_ref[2 * kv + 1, r0:r0 + 2 * BLOCK, :]], axis=0)
        vx = jnp.concatenate([vx_ref[2 * kv, r0:r0 + 2 * BLOCK, :],
                              vx_ref[2 * kv + 1, r0:r0 + 2 * BLOCK, :]], axis=0)
        qs = jnp.concatenate([qb[r0:r0 + BLOCK, c0:c0 + LANES],
                              qb[r0:r0 + BLOCK, c0 + LANES:c0 + 2 * LANES]], axis=0)
        s = lax.dot_general(qs, kx, (((1,), (1,)), ((), ())), preferred_element_type=F32)
        pairs = []
        for pair in range(2):
            probs = []
            for e in range(2):
                head = kv * GQA_GROUP + pair * 2 + e
                sh = s[pair * BLOCK:(pair + 1) * BLOCK, e * 2 * BLOCK:(e + 1) * 2 * BLOCK] + bias_ref[head]
                if n == 0:
                    sh = jnp.where(seq_start_keys, sh, NEG_INF)
                sink = sinks_ref[head] * LOG2E
                m = jnp.maximum(jnp.max(sh, axis=-1, keepdims=True), sink)
                ex = jnp.exp2(sh - m)
                den = jnp.sum(ex, axis=-1, keepdims=True) + jnp.exp2(sink - m)
                probs.append((ex * (1.0 / den)).astype(BF16))
            pairs.append(jnp.concatenate(probs, axis=1))
        o = _dot(jnp.concatenate(pairs, axis=0), vx)
        attn_ref[r0:r0 + BLOCK, c0:c0 + LANES] = o[0:BLOCK].astype(BF16)
        attn_ref[r0:r0 + BLOCK, c0 + LANES:c0 + 2 * LANES] = o[BLOCK:2 * BLOCK].astype(BF16)

    tap0 = CONV_HALO - (CONV_KERNEL - 1)
    rows = ts // PHASES

    def depthwise_conv(j):
        cols = slice(j * LANES, (j + 1) * LANES)
        for p in range(PHASES):
            acc = jnp.broadcast_to(cb_ref[:, cols], (rows, LANES))
            for k in range(CONV_KERNEL):
                acc = acc + cw_ref[k:k + 1, cols] * cbuf_ref[j, pl.ds(tap0 + k + p, rows, stride=PHASES), :]
            ybuf_ref[j, pl.ds(p, rows, stride=PHASES), :] = acc

    units = [(n, kv) for n in range(ts // BLOCK) for kv in range(N_KV_HEADS)]
    for u in range(max(len(units), n_slabs)):
        if u < n_slabs:
            depthwise_conv(u)
        if u < len(units):
            attention(*units[u])

    y = jnp.concatenate([ybuf_ref[j] for j in range(n_slabs)], axis=1)
    y_hi = y.astype(BF16)
    y_lo = (y - y_hi.astype(F32)).astype(BF16)
    mu = _dot(y_hi, seg) + _dot(y_lo, seg)
    d = y - mu
    var = _dot((d * d).astype(BF16), seg)
    yn = d * lax.rsqrt(var + EPS) * cg_ref[...] + cbeta_ref[...]
    conv_out = (yn * _sigmoid(yn)).astype(BF16)

    mixed = jnp.concatenate([attn_ref[...], conv_out], axis=1)
    o_ref[...] = xt + _dot(mixed, wout_ref[...]) + bout_ref[...]

    kx_ref[:, 0:BLOCK, :] = kx_ref[:, ts:ts + BLOCK, :]
    vx_ref[:, 0:BLOCK, :] = vx_ref[:, ts:ts + BLOCK, :]
    cbuf_ref[:, 0:CONV_HALO, :] = cbuf_ref[:, ts:ts + CONV_HALO, :]


def _ffn_kernel(tiles_per_seq, x_ref, xnext_ref, vec_ref, wup_ref, dw_ref, wdown_ref,
                o_ref, ubuf_ref, act_ref, obuf_ref, hbuf_ref):
    ts = x_ref.shape[0]
    d_ff, d_model = wdown_ref.shape
    n_slabs = 2 * d_ff // LANES
    g_ref = vec_ref.at[:, 0:d_model]
    db_ref = vec_ref.at[:, d_model:d_model + 2 * d_ff]
    step = pl.program_id(0)
    first = (step % tiles_per_seq) == 0

    @pl.when(first)
    def _zero_halo():
        ubuf_ref[:, 0:FFN_HALO, :] = jnp.zeros((n_slabs, FFN_HALO, LANES), F32)

    @pl.when(step == 0)
    def _first_norm():
        hbuf_ref[...] = _rms_rows(x_ref[...], g_ref[...]).astype(BF16)

    xt = x_ref[...]
    h = hbuf_ref[...]
    tap0 = FFN_HALO - (FFN_KERNEL - 1)
    rows = ts // PHASES

    def up_chunk(c0):
        up = _dot(h, wup_ref[:, c0:c0 + FFN_CHUNK])
        for s in range(FFN_CHUNK // LANES):
            ubuf_ref[c0 // LANES + s, FFN_HALO:FFN_HALO + ts, :] = up[:, s * LANES:(s + 1) * LANES]

    def conv(c0):
        slabs = []
        for s in range(FFN_CHUNK // LANES):
            cols = slice(c0 + s * LANES, c0 + (s + 1) * LANES)
            phases = []
            for p in range(PHASES):
                acc = jnp.broadcast_to(db_ref[:, cols], (rows, LANES))
                for k in range(FFN_KERNEL):
                    acc = acc + dw_ref[k:k + 1, cols] * ubuf_ref[c0 // LANES + s,
                                                                   pl.ds(tap0 + k + p, rows, stride=PHASES), :]
                phases.append(acc)
            slabs.append(jnp.concatenate(phases, axis=0))
        return jnp.concatenate(slabs, axis=1)

    for j in range(d_ff // FFN_CHUNK):
        cg, cu = j * FFN_CHUNK, d_ff + j * FFN_CHUNK
        up_chunk(cg)
        up_chunk(cu)
        gate = conv(cg)
        act_ref[:, cg:cg + FFN_CHUNK] = (gate * _sigmoid(gate) * conv(cu)).astype(BF16)
        if j == 0:
            h_next = _rms_rows(xnext_ref[...], g_ref[...]).astype(BF16)

    half = (d_ff // FFN_CHUNK // 2 + 1) * FFN_CHUNK
    down = (_dot(act_ref[:, 0:half], wdown_ref[0:half, :])
            + _dot(act_ref[:, half:], wdown_ref[half:, :]))
    for s in range(d_model // LANES):
        for p in range(PHASES):
            obuf_ref[s, pl.ds(p, rows, stride=PHASES), :] = down[p * rows:(p + 1) * rows, s * LANES:(s + 1) * LANES]
    o_ref[...] = xt + jnp.concatenate([obuf_ref[s] for s in range(d_model // LANES)], axis=1)
    ubuf_ref[:, 0:FFN_HALO, :] = ubuf_ref[:, ts:ts + FFN_HALO, :]
    hbuf_ref[...] = h_next


def _const_spec(shape):
    return pl.BlockSpec(shape, lambda i, *_: (0,) * len(shape), pipeline_mode=pl.Buffered(1))


def _band_spec(n_rows, n_cols, n_steps):
    band = n_rows // n_steps
    while band % BF16_ROWS:
        n_steps //= 2
        band = n_rows // n_steps
    assert band * n_steps == n_rows
    return pl.BlockSpec((band, n_cols), lambda i, *_: (jnp.minimum(i, n_steps - 1), 0))


def _segment_mean_matrix(width, group):
    return np.kron(np.eye(width // group), np.full((group, group), 1.0 / group)).astype(np.float32)


def kernel(x, mix_norm_gain, w_in, b_in, q_norm_gain, k_norm_gain, attn_sinks, conv_dw_w, conv_dw_b,
           conv_norm_gain, conv_norm_bias, w_out, b_out, ffn_norm_gain, w_up, ffn_dw_w, ffn_dw_b, w_down):
    batch, seq, d_model = x.shape
    tokens = batch * seq
    attn_w = N_Q_HEADS * HEAD_DIM
    conv_w = conv_dw_w.shape[1]
    d_ff = w_down.shape[0]
    assert seq % MIX_TILE == 0 and seq % FFN_TILE == 0 and d_ff % FFN_CHUNK == 0
    assert conv_w // CONV_GROUPS == HEAD_DIM and attn_w == conv_w

    x2 = x.reshape(tokens, d_model)
    row = lambda v: v.reshape(1, -1).astype(F32)
    seg = jnp.asarray(_segment_mean_matrix(attn_w, HEAD_DIM), BF16)

    mix_vec = jnp.concatenate([mix_norm_gain, b_in, jnp.tile(q_norm_gain, N_Q_HEADS),
                               jnp.tile(k_norm_gain, N_KV_HEADS), conv_dw_b, conv_norm_gain,
                               conv_norm_bias, b_out]).astype(F32)
    mix_consts = (row(mix_vec), w_in.astype(F32), conv_dw_w.astype(F32), w_out.astype(F32), seg)
    n_mix = tokens // MIX_TILE
    tile_spec = lambda t: pl.BlockSpec((t, d_model), lambda i, *_: (i, 0))
    wup_band = _band_spec(d_model, 2 * d_ff, n_mix)
    wdown_band = _band_spec(d_ff, d_model, n_mix)
    x1, w_up_bf, w_down_bf = pl.pallas_call(
        functools.partial(_mix_kernel, seq // MIX_TILE),
        grid_spec=pltpu.PrefetchScalarGridSpec(
            num_scalar_prefetch=1,
            grid=(n_mix,),
            in_specs=[tile_spec(MIX_TILE)] + [_const_spec(a.shape) for a in mix_consts]
                     + [wup_band, wdown_band],
            out_specs=[tile_spec(MIX_TILE), wup_band, wdown_band],
            scratch_shapes=[
                pltpu.VMEM((4, BLOCK + MIX_TILE, LANES), BF16),
                pltpu.VMEM((4, BLOCK + MIX_TILE, LANES), BF16),
                pltpu.VMEM((conv_w // LANES, CONV_HALO + MIX_TILE, LANES), F32),
                pltpu.VMEM((conv_w // LANES, MIX_TILE, LANES), F32),
                pltpu.VMEM((MIX_TILE, attn_w), BF16),
                pltpu.VMEM((N_Q_HEADS, BLOCK, 2 * BLOCK), F32),
                pltpu.VMEM(w_in.shape, BF16),
                pltpu.VMEM(w_out.shape, BF16),
            ]),
        out_shape=[jax.ShapeDtypeStruct((tokens, d_model), F32),
                   jax.ShapeDtypeStruct(w_up.shape, BF16),
                   jax.ShapeDtypeStruct(w_down.shape, BF16)],
        compiler_params=pltpu.CompilerParams(dimension_semantics=("arbitrary",),
                                             vmem_limit_bytes=VMEM_LIMIT_BYTES),
        name="token_mix",
    )(attn_sinks.astype(F32), x2, *mix_consts, w_up.astype(F32), w_down.astype(F32))

    ffn_vec = jnp.concatenate([ffn_norm_gain, ffn_dw_b]).astype(F32)
    ffn_consts = (row(ffn_vec), w_up_bf, ffn_dw_w.astype(F32), w_down_bf)
    n_ffn = tokens // FFN_TILE
    next_tile_spec = pl.BlockSpec((FFN_TILE, d_model), lambda i: (jnp.minimum(i + 1, n_ffn - 1), 0))
    out = pl.pallas_call(
        functools.partial(_ffn_kernel, seq // FFN_TILE),
        grid=(n_ffn,),
        in_specs=[tile_spec(FFN_TILE), next_tile_spec] + [_const_spec(a.shape) for a in ffn_consts],
        out_specs=tile_spec(FFN_TILE),
        scratch_shapes=[
            pltpu.VMEM((2 * d_ff // LANES, FFN_HALO + FFN_TILE, LANES), F32),
            pltpu.VMEM((FFN_TILE, d_ff), BF16),
            pltpu.VMEM((d_model // LANES, FFN_TILE, LANES), F32),
            pltpu.VMEM((FFN_TILE, d_model), BF16),
        ],
        out_shape=jax.ShapeDtypeStruct((tokens, d_model), F32),
        compiler_params=pltpu.CompilerParams(dimension_semantics=("arbitrary",),
                                             vmem_limit_bytes=VMEM_LIMIT_BYTES),
        name="channel_mix",
    )(x1, x1, *ffn_consts)
    return out.reshape(batch, seq, d_model)
```

```python
import functools
import math

import numpy as np
import jax
import jax.numpy as jnp
from jax import lax
from jax.experimental import pallas as pl
from jax.experimental.pallas import tpu as pltpu

F32 = jnp.float32
BF16 = jnp.bfloat16

HEAD_DIM = 64
N_Q_HEADS = 8
N_KV_HEADS = 2
GQA_GROUP = N_Q_HEADS // N_KV_HEADS
BLOCK = 128
CONV_GROUPS = 8
CONV_KERNEL = 31
FFN_KERNEL = 3
EPS = 1e-6
NEG_INF = -1e30
LOG2E = math.log2(math.e)

LANES = 128
SUBLANES = 8
BF16_ROWS = 16
CONV_HALO = 32
FFN_HALO = SUBLANES
VMEM_LIMIT_BYTES = 60 * 1024 * 1024

MIX_TILE = 1024
FFN_TILE = 512
FFN_CHUNK = 256
GLU_CHUNK = 256
CONV_ROWS = 128
PHASES = 4

ALIBI_SLOPES = tuple(float(2.0 ** (-8.0 * (h + 1.0) / N_Q_HEADS)) for h in range(N_Q_HEADS))


def _sigmoid(v):
    return 1.0 / (1.0 + jnp.exp2(v * (-LOG2E)))


def _rms_rows(v, gain):
    ms = jnp.mean(v * v, axis=-1, keepdims=True)
    return v * lax.rsqrt(ms + EPS) * gain


def _dot(a, b):
    return jnp.dot(a, b, preferred_element_type=F32)


def _mix_kernel(tiles_per_seq, sinks_ref, x_ref, g1_ref, win32_ref, bin_ref, gq_ref, gk_ref,
                cw_ref, cb_ref, cg_ref, cbeta_ref, wout32_ref, bout_ref, seg_ref, wup_ref, wdown_ref,
                o_ref, wup_bf_ref, wdown_bf_ref,
                kx_ref, vx_ref, cbuf_ref, ybuf_ref, attn_ref, bias_ref, win_ref, wout_ref):
    ts = x_ref.shape[0]
    attn_w = N_Q_HEADS * HEAD_DIM
    kv_w = N_KV_HEADS * HEAD_DIM
    conv_w = cw_ref.shape[1]
    qkv_w = attn_w + 2 * kv_w
    step = pl.program_id(0)
    first = (step % tiles_per_seq) == 0

    @pl.when(step == 0)
    def _fill_tables():
        win_ref[...] = win32_ref[...].astype(BF16)
        wout_ref[...] = wout32_ref[...].astype(BF16)
        qi = lax.broadcasted_iota(jnp.int32, (BLOCK, 2 * BLOCK), 0)
        kj = lax.broadcasted_iota(jnp.int32, (BLOCK, 2 * BLOCK), 1)
        rel_i = qi + BLOCK - kj
        band = (rel_i >= 0) & (rel_i < BLOCK)
        rel = rel_i.astype(F32)
        for head in range(N_Q_HEADS):
            bias_ref[head] = jnp.where(band, rel * (-ALIBI_SLOPES[head] * LOG2E), NEG_INF)

    @pl.when(first)
    def _zero_halos():
        kx_ref[:, 0:BLOCK, :] = jnp.zeros((4, BLOCK, LANES), BF16)
        vx_ref[:, 0:BLOCK, :] = jnp.zeros((4, BLOCK, LANES), BF16)
        cbuf_ref[:, 0:CONV_HALO, :] = jnp.zeros((conv_w // LANES, CONV_HALO, LANES), F32)

    wup_bf_ref[...] = wup_ref[...].astype(BF16)
    wdown_bf_ref[...] = wdown_ref[...].astype(BF16)

    xt = x_ref[...]
    h = _rms_rows(xt, g1_ref[...]).astype(BF16)
    seg = seg_ref[...]

    n_slabs = conv_w // LANES
    tap0 = CONV_HALO - (CONV_KERNEL - 1)
    rows = ts // PHASES

    def depthwise_conv(j):
        cols = slice(j * LANES, (j + 1) * LANES)
        for p in range(PHASES):
            for r in range(0, rows, CONV_ROWS):
                acc = jnp.broadcast_to(cb_ref[:, cols], (CONV_ROWS, LANES))
                for k in range(CONV_KERNEL):
                    acc = acc + cw_ref[k:k + 1, cols] * cbuf_ref[
                        j, pl.ds(tap0 + k + p + PHASES * r, CONV_ROWS, stride=PHASES), :]
                ybuf_ref[j, pl.ds(p + PHASES * r, CONV_ROWS, stride=PHASES), :] = acc

    def glu_chunk(c):
        a0 = qkv_w + c * GLU_CHUNK
        g0 = qkv_w + conv_w + c * GLU_CHUNK
        a = _dot(h, win_ref[:, a0:a0 + GLU_CHUNK]) + bin_ref[:, a0:a0 + GLU_CHUNK]
        g = _dot(h, win_ref[:, g0:g0 + GLU_CHUNK]) + bin_ref[:, g0:g0 + GLU_CHUNK]
        glu = a * _sigmoid(g)
        for s in range(slabs_per_chunk):
            cbuf_ref[c * slabs_per_chunk + s, CONV_HALO:CONV_HALO + ts, :] = glu[:, s * LANES:(s + 1) * LANES]

    slabs_per_chunk = GLU_CHUNK // LANES
    n_chunks = conv_w // GLU_CHUNK
    glu_chunk(0)
    for c in range(1, n_chunks):
        glu_chunk(c)
        for j in range((c - 1) * slabs_per_chunk, c * slabs_per_chunk):
            depthwise_conv(j)
    late_slabs = list(range((n_chunks - 1) * slabs_per_chunk, n_slabs))

    qkv = _dot(h, win_ref[:, 0:qkv_w]) + bin_ref[:, 0:qkv_w]
    q = qkv[:, 0:attn_w]
    q_ms = _dot((q * q).astype(BF16), seg)
    qn = q * lax.rsqrt(q_ms + EPS) * (gq_ref[...] * (LOG2E / math.sqrt(HEAD_DIM)))
    k2 = qkv[:, attn_w:attn_w + kv_w]
    k_ms = _dot((k2 * k2).astype(BF16), seg[0:kv_w, 0:kv_w])
    kn = k2 * lax.rsqrt(k_ms + EPS) * gk_ref[...]
    v2 = qkv[:, attn_w + kv_w:attn_w + 2 * kv_w]

    lane = lax.broadcasted_iota(jnp.int32, (ts, LANES), 1)
    low = lane < HEAD_DIM
    for src, dst in ((kn, kx_ref), (v2, vx_ref)):
        swapped = pltpu.roll(src, HEAD_DIM, axis=1)
        dst[0, BLOCK:BLOCK + ts, :] = jnp.where(low, src, 0.0).astype(BF16)
        dst[1, BLOCK:BLOCK + ts, :] = jnp.where(low, 0.0, swapped).astype(BF16)
        dst[2, BLOCK:BLOCK + ts, :] = jnp.where(low, swapped, 0.0).astype(BF16)
        dst[3, BLOCK:BLOCK + ts, :] = jnp.where(low, 0.0, src).astype(BF16)
    qb = qn.astype(BF16)

    kj = lax.broadcasted_iota(jnp.int32, (BLOCK, 2 * BLOCK), 1)
    seq_start_keys = kj >= jnp.where(first, BLOCK, 0)

    def attention(n, kv):
        r0 = n * BLOCK
        c0 = kv * GQA_GROUP * HEAD_DIM
        kx = jnp.concatenate([kx_ref[2 * kv, r0:r0 + 2 * BLOCK, :],
                              kx_ref[2 * kv + 1, r0:r0 + 2 * BLOCK, :]], axis=0)
        vx = jnp.concatenate([vx_ref[2 * kv, r0:r0 + 2 * BLOCK, :],
                              vx_ref[2 * kv + 1, r0:r0 + 2 * BLOCK, :]], axis=0)
        qs = jnp.concatenate([qb[r0:r0 + BLOCK, c0:c0 + LANES],
                              qb[r0:r0 + BLOCK, c0 + LANES:c0 + 2 * LANES]], axis=0)
        s = lax.dot_general(qs, kx, (((1,), (1,)), ((), ())), preferred_element_type=F32)
        pairs = []
        for pair in range(2):
            probs = []
            for e in range(2):
                head = kv * GQA_GROUP + pair * 2 + e
                sh = s[pair * BLOCK:(pair + 1) * BLOCK, e * 2 * BLOCK:(e + 1) * 2 * BLOCK] + bias_ref[head]
                if n == 0:
                    sh = jnp.where(seq_start_keys, sh, NEG_INF)
                sink = sinks_ref[head] * LOG2E
                m = jnp.maximum(jnp.max(sh, axis=-1, keepdims=True), sink)
                ex = jnp.exp2(sh - m)
                den = jnp.sum(ex, axis=-1, keepdims=True) + jnp.exp2(sink - m)
                probs.append((ex * (1.0 / den)).astype(BF16))
            pairs.append(jnp.concatenate(probs, axis=1))
        o = _dot(jnp.concatenate(pairs, axis=0), vx)
        attn_ref[r0:r0 + BLOCK, c0:c0 + LANES] = o[0:BLOCK].astype(BF16)
        attn_ref[r0:r0 + BLOCK, c0 + LANES:c0 + 2 * LANES] = o[BLOCK:2 * BLOCK].astype(BF16)

    units = [(n, kv) for n in range(ts // BLOCK) for kv in range(N_KV_HEADS)]
    for u in range(max(len(units), len(late_slabs))):
        if u < len(late_slabs):
            depthwise_conv(late_slabs[u])
        if u < len(units):
            attention(*units[u])

    y = jnp.concatenate([ybuf_ref[j] for j in range(n_slabs)], axis=1)
    y_hi = y.astype(BF16)
    y_lo = (y - y_hi.astype(F32)).astype(BF16)
    mu = _dot(y_hi, seg) + _dot(y_lo, seg)
    d = y - mu
    var = _dot((d * d).astype(BF16), seg)
    yn = d * lax.rsqrt(var + EPS) * cg_ref[...] + cbeta_ref[...]
    conv_out = (yn * _sigmoid(yn)).astype(BF16)

    mixed = jnp.concatenate([attn_ref[...], conv_out], axis=1)
    o_ref[...] = xt + _dot(mixed, wout_ref[...]) + bout_ref[...]

    kx_ref[:, 0:BLOCK, :] = kx_ref[:, ts:ts + BLOCK, :]
    vx_ref[:, 0:BLOCK, :] = vx_ref[:, ts:ts + BLOCK, :]
    cbuf_ref[:, 0:CONV_HALO, :] = cbuf_ref[:, ts:ts + CONV_HALO, :]


def _ffn_kernel(tiles_per_seq, x_ref, xnext_ref, g_ref, wup_ref, dw_ref, db_ref, wdown_ref,
                o_ref, ubuf_ref, act_ref, obuf_ref, hbuf_ref):
    ts = x_ref.shape[0]
    d_ff, d_model = wdown_ref.shape
    n_slabs = 2 * d_ff // LANES
    step = pl.program_id(0)
    first = (step % tiles_per_seq) == 0

    @pl.when(first)
    def _zero_halo():
        ubuf_ref[:, 0:FFN_HALO, :] = jnp.zeros((n_slabs, FFN_HALO, LANES), F32)

    @pl.when(step == 0)
    def _first_norm():
        hbuf_ref[...] = _rms_rows(x_ref[...], g_ref[...]).astype(BF16)

    xt = x_ref[...]
    h = hbuf_ref[...]
    tap0 = FFN_HALO - (FFN_KERNEL - 1)
    rows = ts // PHASES

    def up_chunk(c0):
        up = _dot(h, wup_ref[:, c0:c0 + FFN_CHUNK])
        for s in range(FFN_CHUNK // LANES):
            ubuf_ref[c0 // LANES + s, FFN_HALO:FFN_HALO + ts, :] = up[:, s * LANES:(s + 1) * LANES]

    def conv(c0):
        slabs = []
        for s in range(FFN_CHUNK // LANES):
            cols = slice(c0 + s * LANES, c0 + (s + 1) * LANES)
            phases = []
            for p in range(PHASES):
                acc = jnp.broadcast_to(db_ref[:, cols], (rows, LANES))
                for k in range(FFN_KERNEL):
                    acc = acc + dw_ref[k:k + 1, cols] * ubuf_ref[c0 // LANES + s,
                                                                   pl.ds(tap0 + k + p, rows, stride=PHASES), :]
                phases.append(acc)
            slabs.append(jnp.concatenate(phases, axis=0))
        return jnp.concatenate(slabs, axis=1)

    for j in range(d_ff // FFN_CHUNK):
        cg, cu = j * FFN_CHUNK, d_ff + j * FFN_CHUNK
        up_chunk(cg)
        up_chunk(cu)
        gate = conv(cg)
        act_ref[:, cg:cg + FFN_CHUNK] = (gate * _sigmoid(gate) * conv(cu)).astype(BF16)
        if j == 0:
            h_next = _rms_rows(xnext_ref[...], g_ref[...]).astype(BF16)

    half = (d_ff // FFN_CHUNK // 2 + 1) * FFN_CHUNK
    down = (_dot(act_ref[:, 0:half], wdown_ref[0:half, :])
            + _dot(act_ref[:, half:], wdown_ref[half:, :]))
    for s in range(d_model // LANES):
        for p in range(PHASES):
            obuf_ref[s, pl.ds(p, rows, stride=PHASES), :] = down[p * rows:(p + 1) * rows, s * LANES:(s + 1) * LANES]
    o_ref[...] = xt + jnp.concatenate([obuf_ref[s] for s in range(d_model // LANES)], axis=1)
    ubuf_ref[:, 0:FFN_HALO, :] = ubuf_ref[:, ts:ts + FFN_HALO, :]
    hbuf_ref[...] = h_next


def _const_spec(shape):
    return pl.BlockSpec(shape, lambda i, *_: (0,) * len(shape), pipeline_mode=pl.Buffered(1))


def _band_spec(n_rows, n_cols, n_steps):
    band = n_rows // n_steps
    while band % BF16_ROWS:
        n_steps //= 2
        band = n_rows // n_steps
    assert band * n_steps == n_rows
    return pl.BlockSpec((band, n_cols), lambda i, *_: (jnp.minimum(i, n_steps - 1), 0))


def _segment_mean_matrix(width, group):
    return np.kron(np.eye(width // group), np.full((group, group), 1.0 / group)).astype(np.float32)


def kernel(x, mix_norm_gain, w_in, b_in, q_norm_gain, k_norm_gain, attn_sinks, conv_dw_w, conv_dw_b,
           conv_norm_gain, conv_norm_bias, w_out, b_out, ffn_norm_gain, w_up, ffn_dw_w, ffn_dw_b, w_down):
    batch, seq, d_model = x.shape
    tokens = batch * seq
    attn_w = N_Q_HEADS * HEAD_DIM
    conv_w = conv_dw_w.shape[1]
    d_ff = w_down.shape[0]
    assert seq % MIX_TILE == 0 and seq % FFN_TILE == 0 and d_ff % FFN_CHUNK == 0
    assert conv_w // CONV_GROUPS == HEAD_DIM and attn_w == conv_w

    x2 = x.reshape(tokens, d_model)
    row = lambda v: v.reshape(1, -1).astype(F32)
    seg = jnp.asarray(_segment_mean_matrix(attn_w, HEAD_DIM), BF16)

    mix_consts = (row(mix_norm_gain), w_in.astype(F32), row(b_in),
                  row(jnp.tile(q_norm_gain, N_Q_HEADS)), row(jnp.tile(k_norm_gain, N_KV_HEADS)),
                  conv_dw_w.astype(F32), row(conv_dw_b), row(conv_norm_gain), row(conv_norm_bias),
                  w_out.astype(F32), row(b_out), seg)
    n_mix = tokens // MIX_TILE
    tile_spec = lambda t: pl.BlockSpec((t, d_model), lambda i, *_: (i, 0))
    wup_band = _band_spec(d_model, 2 * d_ff, n_mix)
    wdown_band = _band_spec(d_ff, d_model, n_mix)
    x1, w_up_bf, w_down_bf = pl.pallas_call(
        functools.partial(_mix_kernel, seq // MIX_TILE),
        grid_spec=pltpu.PrefetchScalarGridSpec(
            num_scalar_prefetch=1,
            grid=(n_mix,),
            in_specs=[tile_spec(MIX_TILE)] + [_const_spec(a.shape) for a in mix_consts]
                     + [wup_band, wdown_band],
            out_specs=[tile_spec(MIX_TILE), wup_band, wdown_band],
            scratch_shapes=[
                pltpu.VMEM((4, BLOCK + MIX_TILE, LANES), BF16),
                pltpu.VMEM((4, BLOCK + MIX_TILE, LANES), BF16),
                pltpu.VMEM((conv_w // LANES, CONV_HALO + MIX_TILE, LANES), F32),
                pltpu.VMEM((conv_w // LANES, MIX_TILE, LANES), F32),
                pltpu.VMEM((MIX_TILE, attn_w), BF16),
                pltpu.VMEM((N_Q_HEADS, BLOCK, 2 * BLOCK), F32),
                pltpu.VMEM(w_in.shape, BF16),
                pltpu.VMEM(w_out.shape, BF16),
            ]),
        out_shape=[jax.ShapeDtypeStruct((tokens, d_model), F32),
                   jax.ShapeDtypeStruct(w_up.shape, BF16),
                   jax.ShapeDtypeStruct(w_down.shape, BF16)],
        compiler_params=pltpu.CompilerParams(dimension_semantics=("arbitrary",),
                                             vmem_limit_bytes=VMEM_LIMIT_BYTES),
        name="token_mix",
    )(attn_sinks.astype(F32), x2, *mix_consts, w_up.astype(F32), w_down.astype(F32))

    ffn_consts = (row(ffn_norm_gain), w_up_bf, ffn_dw_w.astype(F32), row(ffn_dw_b), w_down_bf)
    n_ffn = tokens // FFN_TILE
    next_tile_spec = pl.BlockSpec((FFN_TILE, d_model), lambda i: (jnp.minimum(i + 1, n_ffn - 1), 0))
    out = pl.pallas_call(
        functools.partial(_ffn_kernel, seq // FFN_TILE),
        grid=(n_ffn,),
        in_specs=[tile_spec(FFN_TILE), next_tile_spec] + [_const_spec(a.shape) for a in ffn_consts],
        out_specs=tile_spec(FFN_TILE),
        scratch_shapes=[
            pltpu.VMEM((2 * d_ff // LANES, FFN_HALO + FFN_TILE, LANES), F32),
            pltpu.VMEM((FFN_TILE, d_ff), BF16),
            pltpu.VMEM((d_model // LANES, FFN_TILE, LANES), F32),
            pltpu.VMEM((FFN_TILE, d_model), BF16),
        ],
        out_shape=jax.ShapeDtypeStruct((tokens, d_model), F32),
        compiler_params=pltpu.CompilerParams(dimension_semantics=("arbitrary",),
                                             vmem_limit_bytes=VMEM_LIMIT_BYTES),
        name="channel_mix",
    )(x1, x1, *ffn_consts)
    return out.reshape(batch, seq, d_model)
```

```python
import functools
import math

import numpy as np
import jax
import jax.numpy as jnp
from jax import lax
from jax.experimental import pallas as pl
from jax.experimental.pallas import tpu as pltpu

F32 = jnp.float32
BF16 = jnp.bfloat16

HEAD_DIM = 64
N_Q_HEADS = 8
N_KV_HEADS = 2
GQA_GROUP = N_Q_HEADS // N_KV_HEADS
BLOCK = 128
CONV_GROUPS = 8
CONV_KERNEL = 31
FFN_KERNEL = 3
EPS = 1e-6
NEG_INF = -1e30
LOG2E = math.log2(math.e)

LANES = 128
SUBLANES = 8
BF16_ROWS = 16
CONV_HALO = 32
FFN_HALO = SUBLANES
VMEM_LIMIT_BYTES = 60 * 1024 * 1024

MIX_TILE = 1024
FFN_TILE = 512
FFN_CHUNK = 256
GLU_CHUNK = 256
CONV_ROWS = 64
PHASES = 4

ALIBI_SLOPES = tuple(float(2.0 ** (-8.0 * (h + 1.0) / N_Q_HEADS)) for h in range(N_Q_HEADS))


def _sigmoid(v):
    return 1.0 / (1.0 + jnp.exp2(v * (-LOG2E)))


def _rms_rows(v, gain):
    ms = jnp.mean(v * v, axis=-1, keepdims=True)
    return v * lax.rsqrt(ms + EPS) * gain


def _dot(a, b):
    return jnp.dot(a, b, preferred_element_type=F32)


def _mix_kernel(tiles_per_seq, sinks_ref, x_ref, g1_ref, win32_ref, bin_ref, gq_ref, gk_ref,
                cw_ref, cb_ref, cg_ref, cbeta_ref, wout32_ref, bout_ref, seg_ref, wup_ref, wdown_ref,
                o_ref, wup_bf_ref, wdown_bf_ref,
                kx_ref, vx_ref, cbuf_ref, ybuf_ref, attn_ref, bias_ref, win_ref, wout_ref):
    ts = x_ref.shape[0]
    attn_w = N_Q_HEADS * HEAD_DIM
    kv_w = N_KV_HEADS * HEAD_DIM
    conv_w = cw_ref.shape[1]
    qkv_w = attn_w + 2 * kv_w
    step = pl.program_id(0)
    first = (step % tiles_per_seq) == 0

    @pl.when(step == 0)
    def _fill_tables():
        win_ref[...] = win32_ref[...].astype(BF16)
        wout_ref[...] = wout32_ref[...].astype(BF16)
        qi = lax.broadcasted_iota(jnp.int32, (BLOCK, 2 * BLOCK), 0)
        kj = lax.broadcasted_iota(jnp.int32, (BLOCK, 2 * BLOCK), 1)
        rel_i = qi + BLOCK - kj
        band = (rel_i >= 0) & (rel_i < BLOCK)
        rel = rel_i.astype(F32)
        for head in range(N_Q_HEADS):
            bias_ref[head] = jnp.where(band, rel * (-ALIBI_SLOPES[head] * LOG2E), NEG_INF)

    @pl.when(first)
    def _zero_halos():
        kx_ref[:, 0:BLOCK, :] = jnp.zeros((4, BLOCK, LANES), BF16)
        vx_ref[:, 0:BLOCK, :] = jnp.zeros((4, BLOCK, LANES), BF16)
        cbuf_ref[:, 0:CONV_HALO, :] = jnp.zeros((conv_w // LANES, CONV_HALO, LANES), F32)

    wup_bf_ref[...] = wup_ref[...].astype(BF16)
    wdown_bf_ref[...] = wdown_ref[...].astype(BF16)

    xt = x_ref[...]
    h = _rms_rows(xt, g1_ref[...]).astype(BF16)
    seg = seg_ref[...]

    n_slabs = conv_w // LANES
    tap0 = CONV_HALO - (CONV_KERNEL - 1)
    rows = ts // PHASES

    def depthwise_conv(j):
        cols = slice(j * LANES, (j + 1) * LANES)
        for p in range(PHASES):
            for r in range(0, rows, CONV_ROWS):
                acc = jnp.broadcast_to(cb_ref[:, cols], (CONV_ROWS, LANES))
                for k in range(CONV_KERNEL):
                    acc = acc + cw_ref[k:k + 1, cols] * cbuf_ref[
                        j, pl.ds(tap0 + k + p + PHASES * r, CONV_ROWS, stride=PHASES), :]
                ybuf_ref[j, pl.ds(p + PHASES * r, CONV_ROWS, stride=PHASES), :] = acc

    def glu_chunk(c):
        a0 = qkv_w + c * GLU_CHUNK
        g0 = qkv_w + conv_w + c * GLU_CHUNK
        a = _dot(h, win_ref[:, a0:a0 + GLU_CHUNK]) + bin_ref[:, a0:a0 + GLU_CHUNK]
        g = _dot(h, win_ref[:, g0:g0 + GLU_CHUNK]) + bin_ref[:, g0:g0 + GLU_CHUNK]
        glu = a * _sigmoid(g)
        for s in range(slabs_per_chunk):
            cbuf_ref[c * slabs_per_chunk + s, CONV_HALO:CONV_HALO + ts, :] = glu[:, s * LANES:(s + 1) * LANES]

    slabs_per_chunk = GLU_CHUNK // LANES
    n_chunks = conv_w // GLU_CHUNK
    glu_chunk(0)
    for c in range(1, n_chunks):
        glu_chunk(c)
        for j in range((c - 1) * slabs_per_chunk, c * slabs_per_chunk):
            depthwise_conv(j)
    late_slabs = list(range((n_chunks - 1) * slabs_per_chunk, n_slabs))

    qkv = _dot(h, win_ref[:, 0:qkv_w]) + bin_ref[:, 0:qkv_w]
    q = qkv[:, 0:attn_w]
    q_ms = _dot((q * q).astype(BF16), seg)
    qn = q * lax.rsqrt(q_ms + EPS) * (gq_ref[...] * (LOG2E / math.sqrt(HEAD_DIM)))
    k2 = qkv[:, attn_w:attn_w + kv_w]
    k_ms = _dot((k2 * k2).astype(BF16), seg[0:kv_w, 0:kv_w])
    kn = k2 * lax.rsqrt(k_ms + EPS) * gk_ref[...]
    v2 = qkv[:, attn_w + kv_w:attn_w + 2 * kv_w]

    lane = lax.broadcasted_iota(jnp.int32, (ts, LANES), 1)
    low = lane < HEAD_DIM
    for src, dst in ((kn, kx_ref), (v2, vx_ref)):
        swapped = pltpu.roll(src, HEAD_DIM, axis=1)
        dst[0, BLOCK:BLOCK + ts, :] = jnp.where(low, src, 0.0).astype(BF16)
        dst[1, BLOCK:BLOCK + ts, :] = jnp.where(low, 0.0, swapped).astype(BF16)
        dst[2, BLOCK:BLOCK + ts, :] = jnp.where(low, swapped, 0.0).astype(BF16)
        dst[3, BLOCK:BLOCK + ts, :] = jnp.where(low, 0.0, src).astype(BF16)
    qb = qn.astype(BF16)

    kj = lax.broadcasted_iota(jnp.int32, (BLOCK, 2 * BLOCK), 1)
    seq_start_keys = kj >= jnp.where(first, BLOCK, 0)

    def attention(n, kv):
        r0 = n * BLOCK
        c0 = kv * GQA_GROUP * HEAD_DIM
        kx = jnp.concatenate([kx_ref[2 * kv, r0:r0 + 2 * BLOCK, :],
                              kx_ref[2 * kv + 1, r0:r0 + 2 * BLOCK, :]], axis=0)
        vx = jnp.concatenate([vx_ref[2 * kv, r0:r0 + 2 * BLOCK, :],
                              vx_ref[2 * kv + 1, r0:r0 + 2 * BLOCK, :]], axis=0)
        qs = jnp.concatenate([qb[r0:r0 + BLOCK, c0:c0 + LANES],
                              qb[r0:r0 + BLOCK, c0 + LANES:c0 + 2 * LANES]], axis=0)
        s = lax.dot_general(qs, kx, (((1,), (1,)), ((), ())), preferred_element_type=F32)
        pairs = []
        for pair in range(2):
            probs = []
            for e in range(2):
                head = kv * GQA_GROUP + pair * 2 + e
                sh = s[pair * BLOCK:(pair + 1) * BLOCK, e * 2 * BLOCK:(e + 1) * 2 * BLOCK] + bias_ref[head]
                if n == 0:
                    sh = jnp.where(seq_start_keys, sh, NEG_INF)
                sink = sinks_ref[head] * LOG2E
                m = jnp.maximum(jnp.max(sh, axis=-1, keepdims=True), sink)
                ex = jnp.exp2(sh - m)
                den = jnp.sum(ex, axis=-1, keepdims=True) + jnp.exp2(sink - m)
                probs.append((ex * (1.0 / den)).astype(BF16))
            pairs.append(jnp.concatenate(probs, axis=1))
        o = _dot(jnp.concatenate(pairs, axis=0), vx)
        attn_ref[r0:r0 + BLOCK, c0:c0 + LANES] = o[0:BLOCK].astype(BF16)
        attn_ref[r0:r0 + BLOCK, c0 + LANES:c0 + 2 * LANES] = o[BLOCK:2 * BLOCK].astype(BF16)

    units = [(n, kv) for n in range(ts // BLOCK) for kv in range(N_KV_HEADS)]
    for u in range(max(len(units), len(late_slabs))):
        if u < len(late_slabs):
            depthwise_conv(late_slabs[u])
        if u < len(units):
            attention(*units[u])

    y = jnp.concatenate([ybuf_ref[j] for j in range(n_slabs)], axis=1)
    y_hi = y.astype(BF16)
    y_lo = (y - y_hi.astype(F32)).astype(BF16)
    mu = _dot(y_hi, seg) + _dot(y_lo, seg)
    d = y - mu
    var = _dot((d * d).astype(BF16), seg)
    yn = d * lax.rsqrt(var + EPS) * cg_ref[...] + cbeta_ref[...]
    conv_out = (yn * _sigmoid(yn)).astype(BF16)

    mixed = jnp.concatenate([attn_ref[...], conv_out], axis=1)
    o_ref[...] = xt + _dot(mixed, wout_ref[...]) + bout_ref[...]

    kx_ref[:, 0:BLOCK, :] = kx_ref[:, ts:ts + BLOCK, :]
    vx_ref[:, 0:BLOCK, :] = vx_ref[:, ts:ts + BLOCK, :]
    cbuf_ref[:, 0:CONV_HALO, :] = cbuf_ref[:, ts:ts + CONV_HALO, :]


def _ffn_kernel(tiles_per_seq, x_ref, xnext_ref, g_ref, wup_ref, dw_ref, db_ref, wdown_ref,
                o_ref, ubuf_ref, act_ref, obuf_ref, hbuf_ref):
    ts = x_ref.shape[0]
    d_ff, d_model = wdown_ref.shape
    n_slabs = 2 * d_ff // LANES
    step = pl.program_id(0)
    first = (step % tiles_per_seq) == 0

    @pl.when(first)
    def _zero_halo():
        ubuf_ref[:, 0:FFN_HALO, :] = jnp.zeros((n_slabs, FFN_HALO, LANES), F32)

    @pl.when(step == 0)
    def _first_norm():
        hbuf_ref[...] = _rms_rows(x_ref[...], g_ref[...]).astype(BF16)

    xt = x_ref[...]
    h = hbuf_ref[...]
    tap0 = FFN_HALO - (FFN_KERNEL - 1)
    rows = ts // PHASES

    def up_chunk(c0):
        up = _dot(h, wup_ref[:, c0:c0 + FFN_CHUNK])
        for s in range(FFN_CHUNK // LANES):
            ubuf_ref[c0 // LANES + s, FFN_HALO:FFN_HALO + ts, :] = up[:, s * LANES:(s + 1) * LANES]

    def conv(c0):
        slabs = []
        for s in range(FFN_CHUNK // LANES):
            cols = slice(c0 + s * LANES, c0 + (s + 1) * LANES)
            phases = []
            for p in range(PHASES):
                acc = jnp.broadcast_to(db_ref[:, cols], (rows, LANES))
                for k in range(FFN_KERNEL):
                    acc = acc + dw_ref[k:k + 1, cols] * ubuf_ref[c0 // LANES + s,
                                                                   pl.ds(tap0 + k + p, rows, stride=PHASES), :]
                phases.append(acc)
            slabs.append(jnp.concatenate(phases, axis=0))
        return jnp.concatenate(slabs, axis=1)

    for j in range(d_ff // FFN_CHUNK):
        cg, cu = j * FFN_CHUNK, d_ff + j * FFN_CHUNK
        up_chunk(cg)
        up_chunk(cu)
        gate = conv(cg)
        act_ref[:, cg:cg + FFN_CHUNK] = (gate * _sigmoid(gate) * conv(cu)).astype(BF16)
        if j == 0:
            h_next = _rms_rows(xnext_ref[...], g_ref[...]).astype(BF16)

    half = (d_ff // FFN_CHUNK // 2 + 1) * FFN_CHUNK
    down = (_dot(act_ref[:, 0:half], wdown_ref[0:half, :])
            + _dot(act_ref[:, half:], wdown_ref[half:, :]))
    for s in range(d_model // LANES):
        for p in range(PHASES):
            obuf_ref[s, pl.ds(p, rows, stride=PHASES), :] = down[p * rows:(p + 1) * rows, s * LANES:(s + 1) * LANES]
    o_ref[...] = xt + jnp.concatenate([obuf_ref[s] for s in range(d_model // LANES)], axis=1)
    ubuf_ref[:, 0:FFN_HALO, :] = ubuf_ref[:, ts:ts + FFN_HALO, :]
    hbuf_ref[...] = h_next


def _const_spec(shape):
    return pl.BlockSpec(shape, lambda i, *_: (0,) * len(shape), pipeline_mode=pl.Buffered(1))


def _band_spec(n_rows, n_cols, n_steps):
    band = n_rows // n_steps
    while band % BF16_ROWS:
        n_steps //= 2
        band = n_rows // n_steps
    assert band * n_steps == n_rows
    return pl.BlockSpec((band, n_cols), lambda i, *_: (jnp.minimum(i, n_steps - 1), 0))


def _segment_mean_matrix(width, group):
    return np.kron(np.eye(width // group), np.full((group, group), 1.0 / group)).astype(np.float32)


def kernel(x, mix_norm_gain, w_in, b_in, q_norm_gain, k_norm_gain, attn_sinks, conv_dw_w, conv_dw_b,
           conv_norm_gain, conv_norm_bias, w_out, b_out, ffn_norm_gain, w_up, ffn_dw_w, ffn_dw_b, w_down):
    batch, seq, d_model = x.shape
    tokens = batch * seq
    attn_w = N_Q_HEADS * HEAD_DIM
    conv_w = conv_dw_w.shape[1]
    d_ff = w_down.shape[0]
    assert seq % MIX_TILE == 0 and seq % FFN_TILE == 0 and d_ff % FFN_CHUNK == 0
    assert conv_w // CONV_GROUPS == HEAD_DIM and attn_w == conv_w

    x2 = x.reshape(tokens, d_model)
    row = lambda v: v.reshape(1, -1).astype(F32)
    seg = jnp.asarray(_segment_mean_matrix(attn_w, HEAD_DIM), BF16)

    mix_consts = (row(mix_norm_gain), w_in.astype(F32), row(b_in),
                  row(jnp.tile(q_norm_gain, N_Q_HEADS)), row(jnp.tile(k_norm_gain, N_KV_HEADS)),
                  conv_dw_w.astype(F32), row(conv_dw_b), row(conv_norm_gain), row(conv_norm_bias),
                  w_out.astype(F32), row(b_out), seg)
    n_mix = tokens // MIX_TILE
    tile_spec = lambda t: pl.BlockSpec((t, d_model), lambda i, *_: (i, 0))
    wup_band = _band_spec(d_model, 2 * d_ff, n_mix)
    wdown_band = _band_spec(d_ff, d_model, n_mix)
    x1, w_up_bf, w_down_bf = pl.pallas_call(
        functools.partial(_mix_kernel, seq // MIX_TILE),
        grid_spec=pltpu.PrefetchScalarGridSpec(
            num_scalar_prefetch=1,
            grid=(n_mix,),
            in_specs=[tile_spec(MIX_TILE)] + [_const_spec(a.shape) for a in mix_consts]
                     + [wup_band, wdown_band],
            out_specs=[tile_spec(MIX_TILE), wup_band, wdown_band],
            scratch_shapes=[
                pltpu.VMEM((4, BLOCK + MIX_TILE, LANES), BF16),
                pltpu.VMEM((4, BLOCK + MIX_TILE, LANES), BF16),
                pltpu.VMEM((conv_w // LANES, CONV_HALO + MIX_TILE, LANES), F32),
                pltpu.VMEM((conv_w // LANES, MIX_TILE, LANES), F32),
                pltpu.VMEM((MIX_TILE, attn_w), BF16),
                pltpu.VMEM((N_Q_HEADS, BLOCK, 2 * BLOCK), F32),
                pltpu.VMEM(w_in.shape, BF16),
                pltpu.VMEM(w_out.shape, BF16),
            ]),
        out_shape=[jax.ShapeDtypeStruct((tokens, d_model), F32),
                   jax.ShapeDtypeStruct(w_up.shape, BF16),
                   jax.ShapeDtypeStruct(w_down.shape, BF16)],
        compiler_params=pltpu.CompilerParams(dimension_semantics=("arbitrary",),
                                             vmem_limit_bytes=VMEM_LIMIT_BYTES),
        name="token_mix",
    )(attn_sinks.astype(F32), x2, *mix_consts, w_up.astype(F32), w_down.astype(F32))

    ffn_consts = (row(ffn_norm_gain), w_up_bf, ffn_dw_w.astype(F32), row(ffn_dw_b), w_down_bf)
    n_ffn = tokens // FFN_TILE
    next_tile_spec = pl.BlockSpec((FFN_TILE, d_model), lambda i: (jnp.minimum(i + 1, n_ffn - 1), 0))
    out = pl.pallas_call(
        functools.partial(_ffn_kernel, seq // FFN_TILE),
        grid=(n_ffn,),
        in_specs=[tile_spec(FFN_TILE), next_tile_spec] + [_const_spec(a.shape) for a in ffn_consts],
        out_specs=tile_spec(FFN_TILE),
        scratch_shapes=[
            pltpu.VMEM((2 * d_ff // LANES, FFN_HALO + FFN_TILE, LANES), F32),
            pltpu.VMEM((FFN_TILE, d_ff), BF16),
            pltpu.VMEM((d_model // LANES, FFN_TILE, LANES), F32),
            pltpu.VMEM((FFN_TILE, d_model), BF16),
        ],
        out_shape=jax.ShapeDtypeStruct((tokens, d_model), F32),
        compiler_params=pltpu.CompilerParams(dimension_semantics=("arbitrary",),
                                             vmem_limit_bytes=VMEM_LIMIT_BYTES),
        name="channel_mix",
    )(x1, x1, *ffn_consts)
    return out.reshape(batch, seq, d_model)
```

```python
import functools
import math

import numpy as np
import jax
import jax.numpy as jnp
from jax import lax
from jax.experimental import pallas as pl
from jax.experimental.pallas import tpu as pltpu

F32 = jnp.float32
BF16 = jnp.bfloat16

HEAD_DIM = 64
N_Q_HEADS = 8
N_KV_HEADS = 2
GQA_GROUP = N_Q_HEADS // N_KV_HEADS
BLOCK = 128
CONV_GROUPS = 8
CONV_KERNEL = 31
FFN_KERNEL = 3
EPS = 1e-6
NEG_INF = -1e30
LOG2E = math.log2(math.e)

LANES = 128
SUBLANES = 8
BF16_ROWS = 16
CONV_HALO = 32
FFN_HALO = SUBLANES
VMEM_LIMIT_BYTES = 60 * 1024 * 1024

MIX_TILE = 1024
FFN_TILE = 1024
FFN_SUB = 512
FFN_CHUNK = 256
GLU_CHUNK = 256
CONV_ROWS = 128
PHASES = 4

ALIBI_SLOPES = tuple(float(2.0 ** (-8.0 * (h + 1.0) / N_Q_HEADS)) for h in range(N_Q_HEADS))


def _sigmoid(v):
    return 1.0 / (1.0 + jnp.exp2(v * (-LOG2E)))


def _rms_rows(v, gain):
    ms = jnp.mean(v * v, axis=-1, keepdims=True)
    return v * lax.rsqrt(ms + EPS) * gain


def _dot(a, b):
    return jnp.dot(a, b, preferred_element_type=F32)


def _mix_kernel(tiles_per_seq, sinks_ref, x_ref, g1_ref, win32_ref, bin_ref, gq_ref, gk_ref,
                cw_ref, cb_ref, cg_ref, cbeta_ref, wout32_ref, bout_ref, seg_ref, wup_ref, wdown_ref,
                o_ref, wup_bf_ref, wdown_bf_ref,
                kx_ref, vx_ref, cbuf_ref, ybuf_ref, attn_ref, bias_ref, win_ref, wout_ref):
    ts = x_ref.shape[0]
    attn_w = N_Q_HEADS * HEAD_DIM
    kv_w = N_KV_HEADS * HEAD_DIM
    conv_w = cw_ref.shape[1]
    qkv_w = attn_w + 2 * kv_w
    step = pl.program_id(0)
    first = (step % tiles_per_seq) == 0

    @pl.when(step == 0)
    def _fill_tables():
        win_ref[...] = win32_ref[...].astype(BF16)
        wout_ref[...] = wout32_ref[...].astype(BF16)
        qi = lax.broadcasted_iota(jnp.int32, (BLOCK, 2 * BLOCK), 0)
        kj = lax.broadcasted_iota(jnp.int32, (BLOCK, 2 * BLOCK), 1)
        rel_i = qi + BLOCK - kj
        band = (rel_i >= 0) & (rel_i < BLOCK)
        rel = rel_i.astype(F32)
        for head in range(N_Q_HEADS):
            bias_ref[head] = jnp.where(band, rel * (-ALIBI_SLOPES[head] * LOG2E), NEG_INF)

    @pl.when(first)
    def _zero_halos():
        kx_ref[:, 0:BLOCK, :] = jnp.zeros((4, BLOCK, LANES), BF16)
        vx_ref[:, 0:BLOCK, :] = jnp.zeros((4, BLOCK, LANES), BF16)
        cbuf_ref[:, 0:CONV_HALO, :] = jnp.zeros((conv_w // LANES, CONV_HALO, LANES), F32)

    wup_bf_ref[...] = wup_ref[...].astype(BF16)
    wdown_bf_ref[...] = wdown_ref[...].astype(BF16)

    xt = x_ref[...]
    h = _rms_rows(xt, g1_ref[...]).astype(BF16)
    seg = seg_ref[...]

    n_slabs = conv_w // LANES
    tap0 = CONV_HALO - (CONV_KERNEL - 1)
    rows = ts // PHASES

    def depthwise_conv(j):
        cols = slice(j * LANES, (j + 1) * LANES)
        for p in range(PHASES):
            for r in range(0, rows, CONV_ROWS):
                acc = jnp.broadcast_to(cb_ref[:, cols], (CONV_ROWS, LANES))
                for k in range(CONV_KERNEL):
                    acc = acc + cw_ref[k:k + 1, cols] * cbuf_ref[
                        j, pl.ds(tap0 + k + p + PHASES * r, CONV_ROWS, stride=PHASES), :]
                ybuf_ref[j, pl.ds(p + PHASES * r, CONV_ROWS, stride=PHASES), :] = acc

    def glu_chunk(c):
        a0 = qkv_w + c * GLU_CHUNK
        g0 = qkv_w + conv_w + c * GLU_CHUNK
        a = _dot(h, win_ref[:, a0:a0 + GLU_CHUNK]) + bin_ref[:, a0:a0 + GLU_CHUNK]
        g = _dot(h, win_ref[:, g0:g0 + GLU_CHUNK]) + bin_ref[:, g0:g0 + GLU_CHUNK]
        glu = a * _sigmoid(g)
        for s in range(slabs_per_chunk):
            cbuf_ref[c * slabs_per_chunk + s, CONV_HALO:CONV_HALO + ts, :] = glu[:, s * LANES:(s + 1) * LANES]

    slabs_per_chunk = GLU_CHUNK // LANES
    n_chunks = conv_w // GLU_CHUNK
    glu_chunk(0)
    for c in range(1, n_chunks):
        glu_chunk(c)
        for j in range((c - 1) * slabs_per_chunk, c * slabs_per_chunk):
            depthwise_conv(j)
    late_slabs = list(range((n_chunks - 1) * slabs_per_chunk, n_slabs))

    qkv = _dot(h, win_ref[:, 0:qkv_w]) + bin_ref[:, 0:qkv_w]
    q = qkv[:, 0:attn_w]
    q_ms = _dot((q * q).astype(BF16), seg)
    qn = q * lax.rsqrt(q_ms + EPS) * (gq_ref[...] * (LOG2E / math.sqrt(HEAD_DIM)))
    k2 = qkv[:, attn_w:attn_w + kv_w]
    k_ms = _dot((k2 * k2).astype(BF16), seg[0:kv_w, 0:kv_w])
    kn = k2 * lax.rsqrt(k_ms + EPS) * gk_ref[...]
    v2 = qkv[:, attn_w + kv_w:attn_w + 2 * kv_w]

    lane = lax.broadcasted_iota(jnp.int32, (ts, LANES), 1)
    low = lane < HEAD_DIM
    for src, dst in ((kn, kx_ref), (v2, vx_ref)):
        swapped = pltpu.roll(src, HEAD_DIM, axis=1)
        dst[0, BLOCK:BLOCK + ts, :] = jnp.where(low, src, 0.0).astype(BF16)
        dst[1, BLOCK:BLOCK + ts, :] = jnp.where(low, 0.0, swapped).astype(BF16)
        dst[2, BLOCK:BLOCK + ts, :] = jnp.where(low, swapped, 0.0).astype(BF16)
        dst[3, BLOCK:BLOCK + ts, :] = jnp.where(low, 0.0, src).astype(BF16)
    qb = qn.astype(BF16)

    kj = lax.broadcasted_iota(jnp.int32, (BLOCK, 2 * BLOCK), 1)
    seq_start_keys = kj >= jnp.where(first, BLOCK, 0)

    def attention(n, kv):
        r0 = n * BLOCK
        c0 = kv * GQA_GROUP * HEAD_DIM
        kx = jnp.concatenate([kx_ref[2 * kv, r0:r0 + 2 * BLOCK, :],
                              kx_ref[2 * kv + 1, r0:r0 + 2 * BLOCK, :]], axis=0)
        vx = jnp.concatenate([vx_ref[2 * kv, r0:r0 + 2 * BLOCK, :],
                              vx_ref[2 * kv + 1, r0:r0 + 2 * BLOCK, :]], axis=0)
        qs = jnp.concatenate([qb[r0:r0 + BLOCK, c0:c0 + LANES],
                              qb[r0:r0 + BLOCK, c0 + LANES:c0 + 2 * LANES]], axis=0)
        s = lax.dot_general(qs, kx, (((1,), (1,)), ((), ())), preferred_element_type=F32)
        pairs = []
        for pair in range(2):
            probs = []
            for e in range(2):
                head = kv * GQA_GROUP + pair * 2 + e
                sh = s[pair * BLOCK:(pair + 1) * BLOCK, e * 2 * BLOCK:(e + 1) * 2 * BLOCK] + bias_ref[head]
                if n == 0:
                    sh = jnp.where(seq_start_keys, sh, NEG_INF)
                sink = sinks_ref[head] * LOG2E
                m = jnp.maximum(jnp.max(sh, axis=-1, keepdims=True), sink)
                ex = jnp.exp2(sh - m)
                den = jnp.sum(ex, axis=-1, keepdims=True) + jnp.exp2(sink - m)
                probs.append((ex * (1.0 / den)).astype(BF16))
            pairs.append(jnp.concatenate(probs, axis=1))
        o = _dot(jnp.concatenate(pairs, axis=0), vx)
        attn_ref[r0:r0 + BLOCK, c0:c0 + LANES] = o[0:BLOCK].astype(BF16)
        attn_ref[r0:r0 + BLOCK, c0 + LANES:c0 + 2 * LANES] = o[BLOCK:2 * BLOCK].astype(BF16)

    units = [(n, kv) for n in range(ts // BLOCK) for kv in range(N_KV_HEADS)]
    for u in range(max(len(units), len(late_slabs))):
        if u < len(late_slabs):
            depthwise_conv(late_slabs[u])
        if u < len(units):
            attention(*units[u])

    y = jnp.concatenate([ybuf_ref[j] for j in range(n_slabs)], axis=1)
    y_hi = y.astype(BF16)
    y_lo = (y - y_hi.astype(F32)).astype(BF16)
    mu = _dot(y_hi, seg) + _dot(y_lo, seg)
    d = y - mu
    var = _dot((d * d).astype(BF16), seg)
    yn = d * lax.rsqrt(var + EPS) * cg_ref[...] + cbeta_ref[...]
    conv_out = (yn * _sigmoid(yn)).astype(BF16)

    mixed = jnp.concatenate([attn_ref[...], conv_out], axis=1)
    o_ref[...] = xt + _dot(mixed, wout_ref[...]) + bout_ref[...]

    kx_ref[:, 0:BLOCK, :] = kx_ref[:, ts:ts + BLOCK, :]
    vx_ref[:, 0:BLOCK, :] = vx_ref[:, ts:ts + BLOCK, :]
    cbuf_ref[:, 0:CONV_HALO, :] = cbuf_ref[:, ts:ts + CONV_HALO, :]


def _ffn_kernel(tiles_per_seq, x_ref, xnext_ref, g_ref, wup_ref, dw_ref, db_ref, wdown_ref,
                o_ref, ubuf_ref, act_ref, obuf_ref, hbuf_ref):
    ts = x_ref.shape[0]
    sub = hbuf_ref.shape[0]
    d_ff, d_model = wdown_ref.shape
    n_slabs = 2 * d_ff // LANES
    slabs_per_chunk = FFN_CHUNK // LANES
    step = pl.program_id(0)
    first = (step % tiles_per_seq) == 0

    @pl.when(first)
    def _zero_halo():
        ubuf_ref[:, 0:FFN_HALO, :] = jnp.zeros((n_slabs, FFN_HALO, LANES), F32)

    @pl.when(step == 0)
    def _first_norm():
        hbuf_ref[...] = _rms_rows(x_ref[0:sub, :], g_ref[...]).astype(BF16)

    tap0 = FFN_HALO - (FFN_KERNEL - 1)
    rows = sub // PHASES
    h = hbuf_ref[...]

    for part in range(ts // sub):
        r0 = part * sub
        xt = x_ref[r0:r0 + sub, :]
        nxt_rows = x_ref[r0 + sub:r0 + 2 * sub, :] if r0 + sub < ts else None

        def up_chunk(c0):
            up = _dot(h, wup_ref[:, c0:c0 + FFN_CHUNK])
            for s in range(slabs_per_chunk):
                slab = c0 // LANES + s
                if part > 0:
                    ubuf_ref[slab, 0:FFN_HALO, :] = ubuf_ref[slab, sub:sub + FFN_HALO, :]
                ubuf_ref[slab, FFN_HALO:FFN_HALO + sub, :] = up[:, s * LANES:(s + 1) * LANES]

        def conv(c0):
            slabs = []
            for s in range(slabs_per_chunk):
                cols = slice(c0 + s * LANES, c0 + (s + 1) * LANES)
                phases = []
                for p in range(PHASES):
                    acc = jnp.broadcast_to(db_ref[:, cols], (rows, LANES))
                    for k in range(FFN_KERNEL):
                        acc = acc + dw_ref[k:k + 1, cols] * ubuf_ref[c0 // LANES + s,
                                                                       pl.ds(tap0 + k + p, rows, stride=PHASES), :]
                    phases.append(acc)
                slabs.append(jnp.concatenate(phases, axis=0))
            return jnp.concatenate(slabs, axis=1)

        for j in range(d_ff // FFN_CHUNK):
            cg, cu = j * FFN_CHUNK, d_ff + j * FFN_CHUNK
            up_chunk(cg)
            up_chunk(cu)
            gate = conv(cg)
            act_ref[:, cg:cg + FFN_CHUNK] = (gate * _sigmoid(gate) * conv(cu)).astype(BF16)
            if j == 0:
                src = nxt_rows if nxt_rows is not None else xnext_ref[...]
                h_next = _rms_rows(src, g_ref[...]).astype(BF16)

        half = (d_ff // FFN_CHUNK // 2 + 1) * FFN_CHUNK
        down = (_dot(act_ref[:, 0:half], wdown_ref[0:half, :])
                + _dot(act_ref[:, half:], wdown_ref[half:, :]))
        for s in range(d_model // LANES):
            for p in range(PHASES):
                obuf_ref[s, pl.ds(p, rows, stride=PHASES), :] = down[p * rows:(p + 1) * rows,
                                                                      s * LANES:(s + 1) * LANES]
        o_ref[r0:r0 + sub, :] = xt + jnp.concatenate([obuf_ref[s] for s in range(d_model // LANES)], axis=1)
        h = h_next

    ubuf_ref[:, 0:FFN_HALO, :] = ubuf_ref[:, sub:sub + FFN_HALO, :]
    hbuf_ref[...] = h


def _const_spec(shape):
    return pl.BlockSpec(shape, lambda i, *_: (0,) * len(shape), pipeline_mode=pl.Buffered(1))


def _band_spec(n_rows, n_cols, n_steps):
    band = n_rows // n_steps
    while band % BF16_ROWS:
        n_steps //= 2
        band = n_rows // n_steps
    assert band * n_steps == n_rows
    return pl.BlockSpec((band, n_cols), lambda i, *_: (jnp.minimum(i, n_steps - 1), 0))


def _segment_mean_matrix(width, group):
    return np.kron(np.eye(width // group), np.full((group, group), 1.0 / group)).astype(np.float32)


def kernel(x, mix_norm_gain, w_in, b_in, q_norm_gain, k_norm_gain, attn_sinks, conv_dw_w, conv_dw_b,
           conv_norm_gain, conv_norm_bias, w_out, b_out, ffn_norm_gain, w_up, ffn_dw_w, ffn_dw_b, w_down):
    batch, seq, d_model = x.shape
    tokens = batch * seq
    attn_w = N_Q_HEADS * HEAD_DIM
    conv_w = conv_dw_w.shape[1]
    d_ff = w_down.shape[0]
    assert seq % MIX_TILE == 0 and seq % FFN_TILE == 0 and FFN_TILE % FFN_SUB == 0 and d_ff % FFN_CHUNK == 0
    assert conv_w // CONV_GROUPS == HEAD_DIM and attn_w == conv_w

    x2 = x.reshape(tokens, d_model)
    row = lambda v: v.reshape(1, -1).astype(F32)
    seg = jnp.asarray(_segment_mean_matrix(attn_w, HEAD_DIM), BF16)

    mix_consts = (row(mix_norm_gain), w_in.astype(F32), row(b_in),
                  row(jnp.tile(q_norm_gain, N_Q_HEADS)), row(jnp.tile(k_norm_gain, N_KV_HEADS)),
                  conv_dw_w.astype(F32), row(conv_dw_b), row(conv_norm_gain), row(conv_norm_bias),
                  w_out.astype(F32), row(b_out), seg)
    n_mix = tokens // MIX_TILE
    tile_spec = lambda t: pl.BlockSpec((t, d_model), lambda i, *_: (i, 0))
    wup_band = _band_spec(d_model, 2 * d_ff, n_mix)
    wdown_band = _band_spec(d_ff, d_model, n_mix)
    x1, w_up_bf, w_down_bf = pl.pallas_call(
        functools.partial(_mix_kernel, seq // MIX_TILE),
        grid_spec=pltpu.PrefetchScalarGridSpec(
            num_scalar_prefetch=1,
            grid=(n_mix,),
            in_specs=[tile_spec(MIX_TILE)] + [_const_spec(a.shape) for a in mix_consts]
                     + [wup_band, wdown_band],
            out_specs=[tile_spec(MIX_TILE), wup_band, wdown_band],
            scratch_shapes=[
                pltpu.VMEM((4, BLOCK + MIX_TILE, LANES), BF16),
                pltpu.VMEM((4, BLOCK + MIX_TILE, LANES), BF16),
                pltpu.VMEM((conv_w // LANES, CONV_HALO + MIX_TILE, LANES), F32),
                pltpu.VMEM((conv_w // LANES, MIX_TILE, LANES), F32),
                pltpu.VMEM((MIX_TILE, attn_w), BF16),
                pltpu.VMEM((N_Q_HEADS, BLOCK, 2 * BLOCK), F32),
                pltpu.VMEM(w_in.shape, BF16),
                pltpu.VMEM(w_out.shape, BF16),
            ]),
        out_shape=[jax.ShapeDtypeStruct((tokens, d_model), F32),
                   jax.ShapeDtypeStruct(w_up.shape, BF16),
                   jax.ShapeDtypeStruct(w_down.shape, BF16)],
        compiler_params=pltpu.CompilerParams(dimension_semantics=("arbitrary",),
                                             vmem_limit_bytes=VMEM_LIMIT_BYTES),
        name="token_mix",
    )(attn_sinks.astype(F32), x2, *mix_consts, w_up.astype(F32), w_down.astype(F32))

    ffn_consts = (row(ffn_norm_gain), w_up_bf, ffn_dw_w.astype(F32), row(ffn_dw_b), w_down_bf)
    n_ffn = tokens // FFN_TILE
    subs = FFN_TILE // FFN_SUB
    next_tile_spec = pl.BlockSpec((FFN_SUB, d_model), lambda i: (jnp.minimum(subs * (i + 1), subs * n_ffn - 1), 0))
    out = pl.pallas_call(
        functools.partial(_ffn_kernel, seq // FFN_TILE),
        grid=(n_ffn,),
        in_specs=[tile_spec(FFN_TILE), next_tile_spec] + [_const_spec(a.shape) for a in ffn_consts],
        out_specs=tile_spec(FFN_TILE),
        scratch_shapes=[
            pltpu.VMEM((2 * d_ff // LANES, FFN_HALO + FFN_SUB, LANES), F32),
            pltpu.VMEM((FFN_SUB, d_ff), BF16),
            pltpu.VMEM((d_model // LANES, FFN_SUB, LANES), F32),
            pltpu.VMEM((FFN_SUB, d_model), BF16),
        ],
        out_shape=jax.ShapeDtypeStruct((tokens, d_model), F32),
        compiler_params=pltpu.CompilerParams(dimension_semantics=("arbitrary",),
                                             vmem_limit_bytes=VMEM_LIMIT_BYTES),
        name="channel_mix",
    )(x1, x1, *ffn_consts)
    return out.reshape(batch, seq, d_model)
```

```python
import functools
import math

import numpy as np
import jax
import jax.numpy as jnp
from jax import lax
from jax.experimental import pallas as pl
from jax.experimental.pallas import tpu as pltpu

F32 = jnp.float32
BF16 = jnp.bfloat16

HEAD_DIM = 64
N_Q_HEADS = 8
N_KV_HEADS = 2
GQA_GROUP = N_Q_HEADS // N_KV_HEADS
BLOCK = 128
CONV_GROUPS = 8
CONV_KERNEL = 31
FFN_KERNEL = 3
EPS = 1e-6
NEG_INF = -1e30
LOG2E = math.log2(math.e)

LANES = 128
SUBLANES = 8
BF16_ROWS = 16
CONV_HALO = 32
FFN_HALO = SUBLANES
VMEM_LIMIT_BYTES = 60 * 1024 * 1024

MIX_TILE = 1024
FFN_TILE = 512
FFN_CHUNK = 256
GLU_CHUNK = 256
CONV_ROWS = 128
PHASES = 4
MIX_PHASES = 2

ALIBI_SLOPES = tuple(float(2.0 ** (-8.0 * (h + 1.0) / N_Q_HEADS)) for h in range(N_Q_HEADS))


def _sigmoid(v):
    return 1.0 / (1.0 + jnp.exp2(v * (-LOG2E)))


def _rms_rows(v, gain):
    ms = jnp.mean(v * v, axis=-1, keepdims=True)
    return v * lax.rsqrt(ms + EPS) * gain


def _dot(a, b):
    return jnp.dot(a, b, preferred_element_type=F32)


def _mix_kernel(tiles_per_seq, sinks_ref, x_ref, g1_ref, win32_ref, bin_ref, gq_ref, gk_ref,
                cw_ref, cb_ref, cg_ref, cbeta_ref, wout32_ref, bout_ref, seg_ref, wup_ref, wdown_ref,
                o_ref, wup_bf_ref, wdown_bf_ref,
                kx_ref, vx_ref, cbuf_ref, ybuf_ref, attn_ref, bias_ref, win_ref, wout_ref):
    ts = x_ref.shape[0]
    attn_w = N_Q_HEADS * HEAD_DIM
    kv_w = N_KV_HEADS * HEAD_DIM
    conv_w = cw_ref.shape[1]
    qkv_w = attn_w + 2 * kv_w
    step = pl.program_id(0)
    first = (step % tiles_per_seq) == 0

    @pl.when(step == 0)
    def _fill_tables():
        win_ref[...] = win32_ref[...].astype(BF16)
        wout_ref[...] = wout32_ref[...].astype(BF16)
        qi = lax.broadcasted_iota(jnp.int32, (BLOCK, 2 * BLOCK), 0)
        kj = lax.broadcasted_iota(jnp.int32, (BLOCK, 2 * BLOCK), 1)
        rel_i = qi + BLOCK - kj
        band = (rel_i >= 0) & (rel_i < BLOCK)
        rel = rel_i.astype(F32)
        for head in range(N_Q_HEADS):
            bias_ref[head] = jnp.where(band, rel * (-ALIBI_SLOPES[head] * LOG2E), NEG_INF)

    @pl.when(first)
    def _zero_halos():
        kx_ref[:, 0:BLOCK, :] = jnp.zeros((4, BLOCK, LANES), BF16)
        vx_ref[:, 0:BLOCK, :] = jnp.zeros((4, BLOCK, LANES), BF16)
        cbuf_ref[:, 0:CONV_HALO, :] = jnp.zeros((conv_w // LANES, CONV_HALO, LANES), F32)

    wup_bf_ref[...] = wup_ref[...].astype(BF16)
    wdown_bf_ref[...] = wdown_ref[...].astype(BF16)

    xt = x_ref[...]
    h = _rms_rows(xt, g1_ref[...]).astype(BF16)
    seg = seg_ref[...]

    n_slabs = conv_w // LANES
    tap0 = CONV_HALO - (CONV_KERNEL - 1)
    rows = ts // MIX_PHASES

    def depthwise_conv(j):
        cols = slice(j * LANES, (j + 1) * LANES)
        for p in range(MIX_PHASES):
            for r in range(0, rows, CONV_ROWS):
                acc = jnp.broadcast_to(cb_ref[:, cols], (CONV_ROWS, LANES))
                for k in range(CONV_KERNEL):
                    acc = acc + cw_ref[k:k + 1, cols] * cbuf_ref[
                        j, pl.ds(tap0 + k + p + MIX_PHASES * r, CONV_ROWS, stride=MIX_PHASES), :]
                ybuf_ref[j, pl.ds(p + MIX_PHASES * r, CONV_ROWS, stride=MIX_PHASES), :] = acc

    def glu_chunk(c):
        a0 = qkv_w + c * GLU_CHUNK
        g0 = qkv_w + conv_w + c * GLU_CHUNK
        a = _dot(h, win_ref[:, a0:a0 + GLU_CHUNK]) + bin_ref[:, a0:a0 + GLU_CHUNK]
        g = _dot(h, win_ref[:, g0:g0 + GLU_CHUNK]) + bin_ref[:, g0:g0 + GLU_CHUNK]
        glu = a * _sigmoid(g)
        for s in range(slabs_per_chunk):
            cbuf_ref[c * slabs_per_chunk + s, CONV_HALO:CONV_HALO + ts, :] = glu[:, s * LANES:(s + 1) * LANES]

    slabs_per_chunk = GLU_CHUNK // LANES
    n_chunks = conv_w // GLU_CHUNK
    glu_chunk(0)
    for c in range(1, n_chunks):
        glu_chunk(c)
        for j in range((c - 1) * slabs_per_chunk, c * slabs_per_chunk):
            depthwise_conv(j)
    late_slabs = list(range((n_chunks - 1) * slabs_per_chunk, n_slabs))

    qkv = _dot(h, win_ref[:, 0:qkv_w]) + bin_ref[:, 0:qkv_w]
    q = qkv[:, 0:attn_w]
    q_ms = _dot((q * q).astype(BF16), seg)
    qn = q * lax.rsqrt(q_ms + EPS) * (gq_ref[...] * (LOG2E / math.sqrt(HEAD_DIM)))
    k2 = qkv[:, attn_w:attn_w + kv_w]
    k_ms = _dot((k2 * k2).astype(BF16), seg[0:kv_w, 0:kv_w])
    kn = k2 * lax.rsqrt(k_ms + EPS) * gk_ref[...]
    v2 = qkv[:, attn_w + kv_w:attn_w + 2 * kv_w]

    lane = lax.broadcasted_iota(jnp.int32, (ts, LANES), 1)
    low = lane < HEAD_DIM
    for src, dst in ((kn, kx_ref), (v2, vx_ref)):
        swapped = pltpu.roll(src, HEAD_DIM, axis=1)
        dst[0, BLOCK:BLOCK + ts, :] = jnp.where(low, src, 0.0).astype(BF16)
        dst[1, BLOCK:BLOCK + ts, :] = jnp.where(low, 0.0, swapped).astype(BF16)
        dst[2, BLOCK:BLOCK + ts, :] = jnp.where(low, swapped, 0.0).astype(BF16)
        dst[3, BLOCK:BLOCK + ts, :] = jnp.where(low, 0.0, src).astype(BF16)
    qb = qn.astype(BF16)

    kj = lax.broadcasted_iota(jnp.int32, (BLOCK, 2 * BLOCK), 1)
    seq_start_keys = kj >= jnp.where(first, BLOCK, 0)

    def attention(n, kv):
        r0 = n * BLOCK
        c0 = kv * GQA_GROUP * HEAD_DIM
        kx = jnp.concatenate([kx_ref[2 * kv, r0:r0 + 2 * BLOCK, :],
                              kx_ref[2 * kv + 1, r0:r0 + 2 * BLOCK, :]], axis=0)
        vx = jnp.concatenate([vx_ref[2 * kv, r0:r0 + 2 * BLOCK, :],
                              vx_ref[2 * kv + 1, r0:r0 + 2 * BLOCK, :]], axis=0)
        qs = jnp.concatenate([qb[r0:r0 + BLOCK, c0:c0 + LANES],
                              qb[r0:r0 + BLOCK, c0 + LANES:c0 + 2 * LANES]], axis=0)
        s = lax.dot_general(qs, kx, (((1,), (1,)), ((), ())), preferred_element_type=F32)
        pairs = []
        for pair in range(2):
            probs = []
            for e in range(2):
                head = kv * GQA_GROUP + pair * 2 + e
                sh = s[pair * BLOCK:(pair + 1) * BLOCK, e * 2 * BLOCK:(e + 1) * 2 * BLOCK] + bias_ref[head]
                if n == 0:
                    sh = jnp.where(seq_start_keys, sh, NEG_INF)
                sink = sinks_ref[head] * LOG2E
                m = jnp.maximum(jnp.max(sh, axis=-1, keepdims=True), sink)
                ex = jnp.exp2(sh - m)
                den = jnp.sum(ex, axis=-1, keepdims=True) + jnp.exp2(sink - m)
                probs.append((ex * (1.0 / den)).astype(BF16))
            pairs.append(jnp.concatenate(probs, axis=1))
        o = _dot(jnp.concatenate(pairs, axis=0), vx)
        attn_ref[r0:r0 + BLOCK, c0:c0 + LANES] = o[0:BLOCK].astype(BF16)
        attn_ref[r0:r0 + BLOCK, c0 + LANES:c0 + 2 * LANES] = o[BLOCK:2 * BLOCK].astype(BF16)

    units = [(n, kv) for n in range(ts // BLOCK) for kv in range(N_KV_HEADS)]
    for u in range(max(len(units), len(late_slabs))):
        if u < len(late_slabs):
            depthwise_conv(late_slabs[u])
        if u < len(units):
            attention(*units[u])

    y = jnp.concatenate([ybuf_ref[j] for j in range(n_slabs)], axis=1)
    y_hi = y.astype(BF16)
    y_lo = (y - y_hi.astype(F32)).astype(BF16)
    mu = _dot(y_hi, seg) + _dot(y_lo, seg)
    d = y - mu
    var = _dot((d * d).astype(BF16), seg)
    yn = d * lax.rsqrt(var + EPS) * cg_ref[...] + cbeta_ref[...]
    conv_out = (yn * _sigmoid(yn)).astype(BF16)

    mixed = jnp.concatenate([attn_ref[...], conv_out], axis=1)
    o_ref[...] = xt + _dot(mixed, wout_ref[...]) + bout_ref[...]

    kx_ref[:, 0:BLOCK, :] = kx_ref[:, ts:ts + BLOCK, :]
    vx_ref[:, 0:BLOCK, :] = vx_ref[:, ts:ts + BLOCK, :]
    cbuf_ref[:, 0:CONV_HALO, :] = cbuf_ref[:, ts:ts + CONV_HALO, :]


def _ffn_kernel(tiles_per_seq, x_ref, xnext_ref, g_ref, wup_ref, dw_ref, db_ref, wdown_ref,
                o_ref, ubuf_ref, act_ref, obuf_ref, hbuf_ref):
    ts = x_ref.shape[0]
    d_ff, d_model = wdown_ref.shape
    n_slabs = 2 * d_ff // LANES
    step = pl.program_id(0)
    first = (step % tiles_per_seq) == 0

    @pl.when(first)
    def _zero_halo():
        ubuf_ref[:, 0:FFN_HALO, :] = jnp.zeros((n_slabs, FFN_HALO, LANES), F32)

    @pl.when(step == 0)
    def _first_norm():
        hbuf_ref[...] = _rms_rows(x_ref[...], g_ref[...]).astype(BF16)

    xt = x_ref[...]
    h = hbuf_ref[...]
    tap0 = FFN_HALO - (FFN_KERNEL - 1)
    rows = ts // PHASES

    def up_chunk(c0):
        up = _dot(h, wup_ref[:, c0:c0 + FFN_CHUNK])
        for s in range(FFN_CHUNK // LANES):
            ubuf_ref[c0 // LANES + s, FFN_HALO:FFN_HALO + ts, :] = up[:, s * LANES:(s + 1) * LANES]

    def conv(c0):
        slabs = []
        for s in range(FFN_CHUNK // LANES):
            cols = slice(c0 + s * LANES, c0 + (s + 1) * LANES)
            phases = []
            for p in range(PHASES):
                acc = jnp.broadcast_to(db_ref[:, cols], (rows, LANES))
                for k in range(FFN_KERNEL):
                    acc = acc + dw_ref[k:k + 1, cols] * ubuf_ref[c0 // LANES + s,
                                                                   pl.ds(tap0 + k + p, rows, stride=PHASES), :]
                phases.append(acc)
            slabs.append(jnp.concatenate(phases, axis=0))
        return jnp.concatenate(slabs, axis=1)

    for j in range(d_ff // FFN_CHUNK):
        cg, cu = j * FFN_CHUNK, d_ff + j * FFN_CHUNK
        up_chunk(cg)
        up_chunk(cu)
        gate = conv(cg)
        act_ref[:, cg:cg + FFN_CHUNK] = (gate * _sigmoid(gate) * conv(cu)).astype(BF16)
        if j == 0:
            h_next = _rms_rows(xnext_ref[...], g_ref[...]).astype(BF16)

    half = (d_ff // FFN_CHUNK // 2 + 1) * FFN_CHUNK
    down = (_dot(act_ref[:, 0:half], wdown_ref[0:half, :])
            + _dot(act_ref[:, half:], wdown_ref[half:, :]))
    for s in range(d_model // LANES):
        for p in range(PHASES):
            obuf_ref[s, pl.ds(p, rows, stride=PHASES), :] = down[p * rows:(p + 1) * rows, s * LANES:(s + 1) * LANES]
    o_ref[...] = xt + jnp.concatenate([obuf_ref[s] for s in range(d_model // LANES)], axis=1)
    ubuf_ref[:, 0:FFN_HALO, :] = ubuf_ref[:, ts:ts + FFN_HALO, :]
    hbuf_ref[...] = h_next


def _const_spec(shape):
    return pl.BlockSpec(shape, lambda i, *_: (0,) * len(shape), pipeline_mode=pl.Buffered(1))


def _band_spec(n_rows, n_cols, n_steps):
    band = n_rows // n_steps
    while band % BF16_ROWS:
        n_steps //= 2
        band = n_rows // n_steps
    assert band * n_steps == n_rows
    return pl.BlockSpec((band, n_cols), lambda i, *_: (jnp.minimum(i, n_steps - 1), 0))


def _segment_mean_matrix(width, group):
    return np.kron(np.eye(width // group), np.full((group, group), 1.0 / group)).astype(np.float32)


def kernel(x, mix_norm_gain, w_in, b_in, q_norm_gain, k_norm_gain, attn_sinks, conv_dw_w, conv_dw_b,
           conv_norm_gain, conv_norm_bias, w_out, b_out, ffn_norm_gain, w_up, ffn_dw_w, ffn_dw_b, w_down):
    batch, seq, d_model = x.shape
    tokens = batch * seq
    attn_w = N_Q_HEADS * HEAD_DIM
    conv_w = conv_dw_w.shape[1]
    d_ff = w_down.shape[0]
    assert seq % MIX_TILE == 0 and seq % FFN_TILE == 0 and d_ff % FFN_CHUNK == 0
    assert conv_w // CONV_GROUPS == HEAD_DIM and attn_w == conv_w

    x2 = x.reshape(tokens, d_model)
    row = lambda v: v.reshape(1, -1).astype(F32)
    seg = jnp.asarray(_segment_mean_matrix(attn_w, HEAD_DIM), BF16)

    mix_consts = (row(mix_norm_gain), w_in.astype(F32), row(b_in),
                  row(jnp.tile(q_norm_gain, N_Q_HEADS)), row(jnp.tile(k_norm_gain, N_KV_HEADS)),
                  conv_dw_w.astype(F32), row(conv_dw_b), row(conv_norm_gain), row(conv_norm_bias),
                  w_out.astype(F32), row(b_out), seg)
    n_mix = tokens // MIX_TILE
    tile_spec = lambda t: pl.BlockSpec((t, d_model), lambda i, *_: (i, 0))
    wup_band = _band_spec(d_model, 2 * d_ff, n_mix)
    wdown_band = _band_spec(d_ff, d_model, n_mix)
    x1, w_up_bf, w_down_bf = pl.pallas_call(
        functools.partial(_mix_kernel, seq // MIX_TILE),
        grid_spec=pltpu.PrefetchScalarGridSpec(
            num_scalar_prefetch=1,
            grid=(n_mix,),
            in_specs=[tile_spec(MIX_TILE)] + [_const_spec(a.shape) for a in mix_consts]
                     + [wup_band, wdown_band],
            out_specs=[tile_spec(MIX_TILE), wup_band, wdown_band],
            scratch_shapes=[
                pltpu.VMEM((4, BLOCK + MIX_TILE, LANES), BF16),
                pltpu.VMEM((4, BLOCK + MIX_TILE, LANES), BF16),
                pltpu.VMEM((conv_w // LANES, CONV_HALO + MIX_TILE, LANES), F32),
                pltpu.VMEM((conv_w // LANES, MIX_TILE, LANES), F32),
                pltpu.VMEM((MIX_TILE, attn_w), BF16),
                pltpu.VMEM((N_Q_HEADS, BLOCK, 2 * BLOCK), F32),
                pltpu.VMEM(w_in.shape, BF16),
                pltpu.VMEM(w_out.shape, BF16),
            ]),
        out_shape=[jax.ShapeDtypeStruct((tokens, d_model), F32),
                   jax.ShapeDtypeStruct(w_up.shape, BF16),
                   jax.ShapeDtypeStruct(w_down.shape, BF16)],
        compiler_params=pltpu.CompilerParams(dimension_semantics=("arbitrary",),
                                             vmem_limit_bytes=VMEM_LIMIT_BYTES),
        name="token_mix",
    )(attn_sinks.astype(F32), x2, *mix_consts, w_up.astype(F32), w_down.astype(F32))

    ffn_consts = (row(ffn_norm_gain), w_up_bf, ffn_dw_w.astype(F32), row(ffn_dw_b), w_down_bf)
    n_ffn = tokens // FFN_TILE
    next_tile_spec = pl.BlockSpec((FFN_TILE, d_model), lambda i: (jnp.minimum(i + 1, n_ffn - 1), 0))
    out = pl.pallas_call(
        functools.partial(_ffn_kernel, seq // FFN_TILE),
        grid=(n_ffn,),
        in_specs=[tile_spec(FFN_TILE), next_tile_spec] + [_const_spec(a.shape) for a in ffn_consts],
        out_specs=tile_spec(FFN_TILE),
        scratch_shapes=[
            pltpu.VMEM((2 * d_ff // LANES, FFN_HALO + FFN_TILE, LANES), F32),
            pltpu.VMEM((FFN_TILE, d_ff), BF16),
            pltpu.VMEM((d_model // LANES, FFN_TILE, LANES), F32),
            pltpu.VMEM((FFN_TILE, d_model), BF16),
        ],
        out_shape=jax.ShapeDtypeStruct((tokens, d_model), F32),
        compiler_params=pltpu.CompilerParams(dimension_semantics=("arbitrary",),
                                             vmem_limit_bytes=VMEM_LIMIT_BYTES),
        name="channel_mix",
    )(x1, x1, *ffn_consts)
    return out.reshape(batch, seq, d_model)
```

```python
import functools
import math

import numpy as np
import jax
import jax.numpy as jnp
from jax import lax
from jax.experimental import pallas as pl
from jax.experimental.pallas import tpu as pltpu

F32 = jnp.float32
BF16 = jnp.bfloat16

HEAD_DIM = 64
N_Q_HEADS = 8
N_KV_HEADS = 2
GQA_GROUP = N_Q_HEADS // N_KV_HEADS
BLOCK = 128
CONV_GROUPS = 8
CONV_KERNEL = 31
FFN_KERNEL = 3
EPS = 1e-6
NEG_INF = -1e30
LOG2E = math.log2(math.e)

LANES = 128
SUBLANES = 8
BF16_ROWS = 16
CONV_HALO = 32
FFN_HALO = SUBLANES
VMEM_LIMIT_BYTES = 60 * 1024 * 1024

MIX_TILE = 1024
FFN_TILE = 512
FFN_CHUNK = 256
GLU_CHUNK = 256
CONV_ROWS = 128
PHASES = 2
MIX_PHASES = 2

ALIBI_SLOPES = tuple(float(2.0 ** (-8.0 * (h + 1.0) / N_Q_HEADS)) for h in range(N_Q_HEADS))


def _sigmoid(v):
    return 1.0 / (1.0 + jnp.exp2(v * (-LOG2E)))


def _rms_rows(v, gain):
    ms = jnp.mean(v * v, axis=-1, keepdims=True)
    return v * lax.rsqrt(ms + EPS) * gain


def _dot(a, b):
    return jnp.dot(a, b, preferred_element_type=F32)


def _mix_kernel(tiles_per_seq, sinks_ref, x_ref, g1_ref, win32_ref, bin_ref, gq_ref, gk_ref,
                cw_ref, cb_ref, cg_ref, cbeta_ref, wout32_ref, bout_ref, seg_ref, wup_ref, wdown_ref,
                o_ref, wup_bf_ref, wdown_bf_ref,
                kx_ref, vx_ref, cbuf_ref, ybuf_ref, attn_ref, bias_ref, win_ref, wout_ref):
    ts = x_ref.shape[0]
    attn_w = N_Q_HEADS * HEAD_DIM
    kv_w = N_KV_HEADS * HEAD_DIM
    conv_w = cw_ref.shape[1]
    qkv_w = attn_w + 2 * kv_w
    step = pl.program_id(0)
    first = (step % tiles_per_seq) == 0

    @pl.when(step == 0)
    def _fill_tables():
        win_ref[...] = win32_ref[...].astype(BF16)
        wout_ref[...] = wout32_ref[...].astype(BF16)
        qi = lax.broadcasted_iota(jnp.int32, (BLOCK, 2 * BLOCK), 0)
        kj = lax.broadcasted_iota(jnp.int32, (BLOCK, 2 * BLOCK), 1)
        rel_i = qi + BLOCK - kj
        band = (rel_i >= 0) & (rel_i < BLOCK)
        rel = rel_i.astype(F32)
        for head in range(N_Q_HEADS):
            bias_ref[head] = jnp.where(band, rel * (-ALIBI_SLOPES[head] * LOG2E), NEG_INF)

    @pl.when(first)
    def _zero_halos():
        kx_ref[:, 0:BLOCK, :] = jnp.zeros((4, BLOCK, LANES), BF16)
        vx_ref[:, 0:BLOCK, :] = jnp.zeros((4, BLOCK, LANES), BF16)
        cbuf_ref[:, 0:CONV_HALO, :] = jnp.zeros((conv_w // LANES, CONV_HALO, LANES), F32)

    wup_bf_ref[...] = wup_ref[...].astype(BF16)
    wdown_bf_ref[...] = wdown_ref[...].astype(BF16)

    xt = x_ref[...]
    h = _rms_rows(xt, g1_ref[...]).astype(BF16)
    seg = seg_ref[...]

    n_slabs = conv_w // LANES
    tap0 = CONV_HALO - (CONV_KERNEL - 1)
    rows = ts // MIX_PHASES

    def depthwise_conv(j):
        cols = slice(j * LANES, (j + 1) * LANES)
        for p in range(MIX_PHASES):
            for r in range(0, rows, CONV_ROWS):
                acc = jnp.broadcast_to(cb_ref[:, cols], (CONV_ROWS, LANES))
                for k in range(CONV_KERNEL):
                    acc = acc + cw_ref[k:k + 1, cols] * cbuf_ref[
                        j, pl.ds(tap0 + k + p + MIX_PHASES * r, CONV_ROWS, stride=MIX_PHASES), :]
                ybuf_ref[j, pl.ds(p + MIX_PHASES * r, CONV_ROWS, stride=MIX_PHASES), :] = acc

    def glu_chunk(c):
        a0 = qkv_w + c * GLU_CHUNK
        g0 = qkv_w + conv_w + c * GLU_CHUNK
        a = _dot(h, win_ref[:, a0:a0 + GLU_CHUNK]) + bin_ref[:, a0:a0 + GLU_CHUNK]
        g = _dot(h, win_ref[:, g0:g0 + GLU_CHUNK]) + bin_ref[:, g0:g0 + GLU_CHUNK]
        glu = a * _sigmoid(g)
        for s in range(slabs_per_chunk):
            cbuf_ref[c * slabs_per_chunk + s, CONV_HALO:CONV_HALO + ts, :] = glu[:, s * LANES:(s + 1) * LANES]

    slabs_per_chunk = GLU_CHUNK // LANES
    n_chunks = conv_w // GLU_CHUNK
    glu_chunk(0)
    for c in range(1, n_chunks):
        glu_chunk(c)
        for j in range((c - 1) * slabs_per_chunk, c * slabs_per_chunk):
            depthwise_conv(j)
    late_slabs = list(range((n_chunks - 1) * slabs_per_chunk, n_slabs))

    qkv = _dot(h, win_ref[:, 0:qkv_w]) + bin_ref[:, 0:qkv_w]
    q = qkv[:, 0:attn_w]
    q_ms = _dot((q * q).astype(BF16), seg)
    qn = q * lax.rsqrt(q_ms + EPS) * (gq_ref[...] * (LOG2E / math.sqrt(HEAD_DIM)))
    k2 = qkv[:, attn_w:attn_w + kv_w]
    k_ms = _dot((k2 * k2).astype(BF16), seg[0:kv_w, 0:kv_w])
    kn = k2 * lax.rsqrt(k_ms + EPS) * gk_ref[...]
    v2 = qkv[:, attn_w + kv_w:attn_w + 2 * kv_w]

    lane = lax.broadcasted_iota(jnp.int32, (ts, LANES), 1)
    low = lane < HEAD_DIM
    for src, dst in ((kn, kx_ref), (v2, vx_ref)):
        swapped = pltpu.roll(src, HEAD_DIM, axis=1)
        dst[0, BLOCK:BLOCK + ts, :] = jnp.where(low, src, 0.0).astype(BF16)
        dst[1, BLOCK:BLOCK + ts, :] = jnp.where(low, 0.0, swapped).astype(BF16)
        dst[2, BLOCK:BLOCK + ts, :] = jnp.where(low, swapped, 0.0).astype(BF16)
        dst[3, BLOCK:BLOCK + ts, :] = jnp.where(low, 0.0, src).astype(BF16)
    qb = qn.astype(BF16)

    kj = lax.broadcasted_iota(jnp.int32, (BLOCK, 2 * BLOCK), 1)
    seq_start_keys = kj >= jnp.where(first, BLOCK, 0)

    def attention(n, kv):
        r0 = n * BLOCK
        c0 = kv * GQA_GROUP * HEAD_DIM
        kx = jnp.concatenate([kx_ref[2 * kv, r0:r0 + 2 * BLOCK, :],
                              kx_ref[2 * kv + 1, r0:r0 + 2 * BLOCK, :]], axis=0)
        vx = jnp.concatenate([vx_ref[2 * kv, r0:r0 + 2 * BLOCK, :],
                              vx_ref[2 * kv + 1, r0:r0 + 2 * BLOCK, :]], axis=0)
        qs = jnp.concatenate([qb[r0:r0 + BLOCK, c0:c0 + LANES],
                              qb[r0:r0 + BLOCK, c0 + LANES:c0 + 2 * LANES]], axis=0)
        s = lax.dot_general(qs, kx, (((1,), (1,)), ((), ())), preferred_element_type=F32)
        pairs = []
        for pair in range(2):
            probs = []
            for e in range(2):
                head = kv * GQA_GROUP + pair * 2 + e
                sh = s[pair * BLOCK:(pair + 1) * BLOCK, e * 2 * BLOCK:(e + 1) * 2 * BLOCK] + bias_ref[head]
                if n == 0:
                    sh = jnp.where(seq_start_keys, sh, NEG_INF)
                sink = sinks_ref[head] * LOG2E
                m = jnp.maximum(jnp.max(sh, axis=-1, keepdims=True), sink)
                ex = jnp.exp2(sh - m)
                den = jnp.sum(ex, axis=-1, keepdims=True) + jnp.exp2(sink - m)
                probs.append((ex * (1.0 / den)).astype(BF16))
            pairs.append(jnp.concatenate(probs, axis=1))
        o = _dot(jnp.concatenate(pairs, axis=0), vx)
        attn_ref[r0:r0 + BLOCK, c0:c0 + LANES] = o[0:BLOCK].astype(BF16)
        attn_ref[r0:r0 + BLOCK, c0 + LANES:c0 + 2 * LANES] = o[BLOCK:2 * BLOCK].astype(BF16)

    units = [(n, kv) for n in range(ts // BLOCK) for kv in range(N_KV_HEADS)]
    for u in range(max(len(units), len(late_slabs))):
        if u < len(late_slabs):
            depthwise_conv(late_slabs[u])
        if u < len(units):
            attention(*units[u])

    y = jnp.concatenate([ybuf_ref[j] for j in range(n_slabs)], axis=1)
    y_hi = y.astype(BF16)
    y_lo = (y - y_hi.astype(F32)).astype(BF16)
    mu = _dot(y_hi, seg) + _dot(y_lo, seg)
    d = y - mu
    var = _dot((d * d).astype(BF16), seg)
    yn = d * lax.rsqrt(var + EPS) * cg_ref[...] + cbeta_ref[...]
    conv_out = (yn * _sigmoid(yn)).astype(BF16)

    mixed = jnp.concatenate([attn_ref[...], conv_out], axis=1)
    o_ref[...] = xt + _dot(mixed, wout_ref[...]) + bout_ref[...]

    kx_ref[:, 0:BLOCK, :] = kx_ref[:, ts:ts + BLOCK, :]
    vx_ref[:, 0:BLOCK, :] = vx_ref[:, ts:ts + BLOCK, :]
    cbuf_ref[:, 0:CONV_HALO, :] = cbuf_ref[:, ts:ts + CONV_HALO, :]


def _ffn_kernel(tiles_per_seq, x_ref, xnext_ref, g_ref, wup_ref, dw_ref, db_ref, wdown_ref,
                o_ref, ubuf_ref, act_ref, obuf_ref, hbuf_ref):
    ts = x_ref.shape[0]
    d_ff, d_model = wdown_ref.shape
    n_slabs = 2 * d_ff // LANES
    step = pl.program_id(0)
    first = (step % tiles_per_seq) == 0

    @pl.when(first)
    def _zero_halo():
        ubuf_ref[:, 0:FFN_HALO, :] = jnp.zeros((n_slabs, FFN_HALO, LANES), F32)

    @pl.when(step == 0)
    def _first_norm():
        hbuf_ref[...] = _rms_rows(x_ref[...], g_ref[...]).astype(BF16)

    xt = x_ref[...]
    h = hbuf_ref[...]
    tap0 = FFN_HALO - (FFN_KERNEL - 1)
    rows = ts // PHASES

    def up_chunk(c0):
        up = _dot(h, wup_ref[:, c0:c0 + FFN_CHUNK])
        for s in range(FFN_CHUNK // LANES):
            ubuf_ref[c0 // LANES + s, FFN_HALO:FFN_HALO + ts, :] = up[:, s * LANES:(s + 1) * LANES]

    def conv(c0):
        slabs = []
        for s in range(FFN_CHUNK // LANES):
            cols = slice(c0 + s * LANES, c0 + (s + 1) * LANES)
            phases = []
            for p in range(PHASES):
                acc = jnp.broadcast_to(db_ref[:, cols], (rows, LANES))
                for k in range(FFN_KERNEL):
                    acc = acc + dw_ref[k:k + 1, cols] * ubuf_ref[c0 // LANES + s,
                                                                   pl.ds(tap0 + k + p, rows, stride=PHASES), :]
                phases.append(acc)
            slabs.append(jnp.concatenate(phases, axis=0))
        return jnp.concatenate(slabs, axis=1)

    for j in range(d_ff // FFN_CHUNK):
        cg, cu = j * FFN_CHUNK, d_ff + j * FFN_CHUNK
        up_chunk(cg)
        up_chunk(cu)
        gate = conv(cg)
        act_ref[:, cg:cg + FFN_CHUNK] = (gate * _sigmoid(gate) * conv(cu)).astype(BF16)
        if j == 0:
            h_next = _rms_rows(xnext_ref[...], g_ref[...]).astype(BF16)

    half = (d_ff // FFN_CHUNK // 2 + 1) * FFN_CHUNK
    down = (_dot(act_ref[:, 0:half], wdown_ref[0:half, :])
            + _dot(act_ref[:, half:], wdown_ref[half:, :]))
    for s in range(d_model // LANES):
        for p in range(PHASES):
            obuf_ref[s, pl.ds(p, rows, stride=PHASES), :] = down[p * rows:(p + 1) * rows, s * LANES:(s + 1) * LANES]
    o_ref[...] = xt + jnp.concatenate([obuf_ref[s] for s in range(d_model // LANES)], axis=1)
    ubuf_ref[:, 0:FFN_HALO, :] = ubuf_ref[:, ts:ts + FFN_HALO, :]
    hbuf_ref[...] = h_next


def _const_spec(shape):
    return pl.BlockSpec(shape, lambda i, *_: (0,) * len(shape), pipeline_mode=pl.Buffered(1))


def _band_spec(n_rows, n_cols, n_steps):
    band = n_rows // n_steps
    while band % BF16_ROWS:
        n_steps //= 2
        band = n_rows // n_steps
    assert band * n_steps == n_rows
    return pl.BlockSpec((band, n_cols), lambda i, *_: (jnp.minimum(i, n_steps - 1), 0))


def _segment_mean_matrix(width, group):
    return np.kron(np.eye(width // group), np.full((group, group), 1.0 / group)).astype(np.float32)


def kernel(x, mix_norm_gain, w_in, b_in, q_norm_gain, k_norm_gain, attn_sinks, conv_dw_w, conv_dw_b,
           conv_norm_gain, conv_norm_bias, w_out, b_out, ffn_norm_gain, w_up, ffn_dw_w, ffn_dw_b, w_down):
    batch, seq, d_model = x.shape
    tokens = batch * seq
    attn_w = N_Q_HEADS * HEAD_DIM
    conv_w = conv_dw_w.shape[1]
    d_ff = w_down.shape[0]
    assert seq % MIX_TILE == 0 and seq % FFN_TILE == 0 and d_ff % FFN_CHUNK == 0
    assert conv_w // CONV_GROUPS == HEAD_DIM and attn_w == conv_w

    x2 = x.reshape(tokens, d_model)
    row = lambda v: v.reshape(1, -1).astype(F32)
    seg = jnp.asarray(_segment_mean_matrix(attn_w, HEAD_DIM), BF16)

    mix_consts = (row(mix_norm_gain), w_in.astype(F32), row(b_in),
                  row(jnp.tile(q_norm_gain, N_Q_HEADS)), row(jnp.tile(k_norm_gain, N_KV_HEADS)),
                  conv_dw_w.astype(F32), row(conv_dw_b), row(conv_norm_gain), row(conv_norm_bias),
                  w_out.astype(F32), row(b_out), seg)
    n_mix = tokens // MIX_TILE
    tile_spec = lambda t: pl.BlockSpec((t, d_model), lambda i, *_: (i, 0))
    wup_band = _band_spec(d_model, 2 * d_ff, n_mix)
    wdown_band = _band_spec(d_ff, d_model, n_mix)
    x1, w_up_bf, w_down_bf = pl.pallas_call(
        functools.partial(_mix_kernel, seq // MIX_TILE),
        grid_spec=pltpu.PrefetchScalarGridSpec(
            num_scalar_prefetch=1,
            grid=(n_mix,),
            in_specs=[tile_spec(MIX_TILE)] + [_const_spec(a.shape) for a in mix_consts]
                     + [wup_band, wdown_band],
            out_specs=[tile_spec(MIX_TILE), wup_band, wdown_band],
            scratch_shapes=[
                pltpu.VMEM((4, BLOCK + MIX_TILE, LANES), BF16),
                pltpu.VMEM((4, BLOCK + MIX_TILE, LANES), BF16),
                pltpu.VMEM((conv_w // LANES, CONV_HALO + MIX_TILE, LANES), F32),
                pltpu.VMEM((conv_w // LANES, MIX_TILE, LANES), F32),
                pltpu.VMEM((MIX_TILE, attn_w), BF16),
                pltpu.VMEM((N_Q_HEADS, BLOCK, 2 * BLOCK), F32),
                pltpu.VMEM(w_in.shape, BF16),
                pltpu.VMEM(w_out.shape, BF16),
            ]),
        out_shape=[jax.ShapeDtypeStruct((tokens, d_model), F32),
                   jax.ShapeDtypeStruct(w_up.shape, BF16),
                   jax.ShapeDtypeStruct(w_down.shape, BF16)],
        compiler_params=pltpu.CompilerParams(dimension_semantics=("arbitrary",),
                                             vmem_limit_bytes=VMEM_LIMIT_BYTES),
        name="token_mix",
    )(attn_sinks.astype(F32), x2, *mix_consts, w_up.astype(F32), w_down.astype(F32))

    ffn_consts = (row(ffn_norm_gain), w_up_bf, ffn_dw_w.astype(F32), row(ffn_dw_b), w_down_bf)
    n_ffn = tokens // FFN_TILE
    next_tile_spec = pl.BlockSpec((FFN_TILE, d_model), lambda i: (jnp.minimum(i + 1, n_ffn - 1), 0))
    out = pl.pallas_call(
        functools.partial(_ffn_kernel, seq // FFN_TILE),
        grid=(n_ffn,),
        in_specs=[tile_spec(FFN_TILE), next_tile_spec] + [_const_spec(a.shape) for a in ffn_consts],
        out_specs=tile_spec(FFN_TILE),
        scratch_shapes=[
            pltpu.VMEM((2 * d_ff // LANES, FFN_HALO + FFN_TILE, LANES), F32),
            pltpu.VMEM((FFN_TILE, d_ff), BF16),
            pltpu.VMEM((d_model // LANES, FFN_TILE, LANES), F32),
            pltpu.VMEM((FFN_TILE, d_model), BF16),
        ],
        out_shape=jax.ShapeDtypeStruct((tokens, d_model), F32),
        compiler_params=pltpu.CompilerParams(dimension_semantics=("arbitrary",),
                                             vmem_limit_bytes=VMEM_LIMIT_BYTES),
        name="channel_mix",
    )(x1, x1, *ffn_consts)
    return out.reshape(batch, seq, d_model)
```
